```python
import jax, jax.numpy as jnp
from jax import lax
import numpy as np

D_MODEL = 2048
BATCH = 2
SEQ = 4096
DEPTH = 1

CONV_CH = D_MODEL // 2
CONV_K = 31
HEAD_DIM = 128
HEADS_PER_GROUP = 4
DILATED_GROUPS = ((128, 1), (512, 4), (2048, 16))
N_GROUPS = len(DILATED_GROUPS)
N_ATTN_HEADS = HEADS_PER_GROUP * N_GROUPS
ATTN_WIDTH = N_ATTN_HEADS * HEAD_DIM
ATTN_OUT_WIDTH = HEADS_PER_GROUP * HEAD_DIM
BAND_BLOCK = 128
OFF_Q = 2 * CONV_CH
OFF_K = OFF_Q + ATTN_WIDTH
OFF_V = OFF_K + ATTN_WIDTH
OFF_G = OFF_V + ATTN_WIDTH
IN_COLS = OFF_G + 2 * D_MODEL
PEER_HEADS = 8
PEER_NKEYS = 128
PEER_EXPERTS = PEER_NKEYS * PEER_NKEYS
PEER_TOPK = 16
PEER_QDIM = 256
PEER_HALF = PEER_QDIM // 2
PEER_CHUNK = 128
EPS = 1e-6

kernel_name = "hybrid_conv_dilatedattn_peer_adaln"


def rmsnorm(x, g):
    xf = x.astype(jnp.float32)
    y = xf * lax.rsqrt(jnp.mean(xf * xf, axis=-1, keepdims=True) + EPS)
    return (y * g.astype(jnp.float32)).astype(x.dtype)


def layernorm(x, g, b):
    xf = x.astype(jnp.float32)
    mu = jnp.mean(xf, axis=-1, keepdims=True)
    var = jnp.mean(jnp.square(xf - mu), axis=-1, keepdims=True)
    y = (xf - mu) * lax.rsqrt(var + EPS) * g.astype(jnp.float32) + b.astype(jnp.float32)
    return y.astype(x.dtype)


def dilated_group_attention(q, k, v, window, dil):
    B, S, Hg, E = q.shape
    w_sub = window // dil
    blk = BAND_BLOCK
    assert w_sub <= blk
    L = S // dil
    nb = -(-L // blk)
    Lp = nb * blk

    def split(t):
        t = t.reshape(B, L, dil, Hg, E)
        return jnp.pad(t, ((0, 0), (0, Lp - L), (0, 0), (0, 0), (0, 0)))

    def band(t):
        t = jnp.pad(split(t), ((0, 0), (blk, 0), (0, 0), (0, 0), (0, 0)))
        t = t.reshape(B, nb + 1, blk, dil, Hg, E)
        return jnp.concatenate([t[:, :-1], t[:, 1:]], axis=2)

    qs = split(q).reshape(B, nb, blk, dil, Hg, E)
    kb, vb = band(k), band(v)
    s = jnp.einsum('bnqrhe,bnkrhe->bnrhqk', qs, kb) * (E ** -0.5)
    qi = jnp.arange(blk)[:, None]
    ki = jnp.arange(2 * blk)[None, :]
    dist = qi + blk - ki
    key_pos = jnp.arange(nb)[:, None, None] * blk + ki[None] - blk
    valid = (dist >= 0)[None] & (dist <= w_sub)[None] & (key_pos >= 0)
    s = jnp.where(valid[None, :, None, None], s, -jnp.inf)
    m = jnp.max(s, axis=-1, keepdims=True)
    p = jnp.exp(s - m)
    den = jnp.sum(p, axis=-1, keepdims=True)
    o = jnp.einsum('bnrhqk,bnkrhe->bnqrhe', p / den, vb)
    lse = (m + jnp.log(den))[..., 0]
    o = o.reshape(B, Lp, dil, Hg, E)[:, :L].reshape(B, S, Hg, E)
    lse = lse.transpose(0, 1, 4, 2, 3).reshape(B, Lp, dil, Hg)[:, :L].reshape(B, S, Hg)
    return o, lse


def token_mixer(h, w_in, conv_dw, conv_db, conv_ln_g, conv_ln_b, w_conv_out, b_conv_out,
                q_norm_g, k_norm_g, w_attn_o, w_out):
    B, S, D = h.shape
    z = h @ w_in
    glu = z[..., :OFF_Q]
    u = glu[..., :CONV_CH] * jax.nn.sigmoid(glu[..., CONV_CH:])
    u = lax.conv_general_dilated(u, conv_dw[:, None, :].astype(u.dtype), window_strides=(1,),
                                 padding=[(CONV_K - 1, 0)], dimension_numbers=('NWC', 'WIO', 'NWC'),
                                 feature_group_count=CONV_CH) + conv_db
    u = jax.nn.silu(layernorm(u, conv_ln_g, conv_ln_b))
    y_conv = u @ w_conv_out + b_conv_out
    def heads(t, g):
        t = t.reshape(B, S, N_ATTN_HEADS, HEAD_DIM).astype(jnp.float32)
        if g is None:
            return t
        return t * lax.rsqrt(jnp.mean(t * t, axis=-1, keepdims=True) + EPS) * g.astype(jnp.float32)
    q = heads(z[..., OFF_Q:OFF_K], q_norm_g)
    k = heads(z[..., OFF_K:OFF_V], k_norm_g)
    v = heads(z[..., OFF_V:OFF_G], None)
    outs, lses = [], []
    for gi, (window, dil) in enumerate(DILATED_GROUPS):
        hs = slice(gi * HEADS_PER_GROUP, (gi + 1) * HEADS_PER_GROUP)
        o, l = dilated_group_attention(q[:, :, hs], k[:, :, hs], v[:, :, hs], window, dil)
        outs.append(o)
        lses.append(l)
    wts = jax.nn.softmax(jnp.stack(lses, axis=0), axis=0)
    o = jnp.sum(wts[..., None] * jnp.stack(outs, axis=0), axis=0)
    y_attn = o.reshape(B, S, ATTN_OUT_WIDTH).astype(h.dtype) @ w_attn_o
    gates = jax.nn.sigmoid(z[..., OFF_G:])
    merged = gates[..., :D] * y_conv + gates[..., D:] * y_attn
    return merged @ w_out


def peer(h, w_q, sub_keys, w_up, w_down):
    B, S, D = h.shape
    qp = (h @ w_q).reshape(B, S, PEER_HEADS, 2, PEER_HALF).astype(jnp.float32)
    s = jnp.einsum('bshpe,hpne->bshpn', qp, sub_keys.astype(jnp.float32))
    v1, i1 = lax.top_k(s[..., 0, :], PEER_TOPK)
    v2, i2 = lax.top_k(s[..., 1, :], PEER_TOPK)
    cand = (v1[..., :, None] + v2[..., None, :]).reshape(B, S, PEER_HEADS, PEER_TOPK * PEER_TOPK)
    top, ci = lax.top_k(cand, PEER_TOPK)
    e1 = jnp.take_along_axis(i1, ci // PEER_TOPK, axis=-1)
    e2 = jnp.take_along_axis(i2, ci % PEER_TOPK, axis=-1)
    idx = e1 * PEER_NKEYS + e2
    g = jax.nn.softmax(top, axis=-1).astype(h.dtype)
    nc = (B * S) // PEER_CHUNK
    hk = PEER_HEADS * PEER_TOPK
    xs = (h.reshape(nc, PEER_CHUNK, D), idx.reshape(nc, PEER_CHUNK, hk), g.reshape(nc, PEER_CHUNK, hk))

    def chunk(args):
        xc, ic, gc = args
        u = jnp.take(w_up, ic, axis=0)
        a = jax.nn.gelu(jnp.einsum('td,tkd->tk', xc, u), approximate=False)
        return jnp.einsum('tk,tkd->td', gc * a, jnp.take(w_down, ic, axis=0))

    return lax.map(chunk, xs).reshape(B, S, D)


def setup_inputs(seed: int = 0) -> dict:
    key = jax.random.key(seed)
    ks = jax.random.split(key, 24)
    D, L = D_MODEL, DEPTH
    nrm = lambda k, shape, s: jax.random.normal(k, shape, jnp.float32) * s
    return {
        "x": nrm(ks[0], (BATCH, SEQ, D), 1.0),
        "c": nrm(ks[1], (BATCH, D), 1.0),
        "norm1_g": 1.0 + nrm(ks[2], (L, D), 0.02),
        "norm2_g": 1.0 + nrm(ks[3], (L, D), 0.02),
        "w_ada": nrm(ks[4], (L, D, 6 * D), 0.5 * D ** -0.5),
        "b_ada": nrm(ks[5], (L, 6 * D), 0.01),
        "w_in": nrm(ks[6], (L, D, IN_COLS), D ** -0.5),
        "conv_dw": nrm(ks[7], (L, CONV_K, CONV_CH), CONV_K ** -0.5),
        "conv_db": nrm(ks[8], (L, CONV_CH), 0.01),
        "conv_ln_g": 1.0 + nrm(ks[9], (L, CONV_CH), 0.02),
        "conv_ln_b": nrm(ks[10], (L, CONV_CH), 0.01),
        "w_conv_out": nrm(ks[11], (L, CONV_CH, D), CONV_CH ** -0.5),
        "b_conv_out": nrm(ks[12], (L, D), 0.01),
        "q_norm_g": 1.0 + nrm(ks[13], (L, N_ATTN_HEADS, HEAD_DIM), 0.02),
        "k_norm_g": 1.0 + nrm(ks[14], (L, N_ATTN_HEADS, HEAD_DIM), 0.02),
        "w_attn_o": nrm(ks[15], (L, ATTN_OUT_WIDTH, D), ATTN_OUT_WIDTH ** -0.5),
        "w_out": nrm(ks[16], (L, D, D), D ** -0.5),
        "peer_w_q": nrm(ks[17], (L, D, PEER_HEADS * PEER_QDIM), D ** -0.5),
        "peer_sub_keys": nrm(ks[18], (L, PEER_HEADS, 2, PEER_NKEYS, PEER_HALF), PEER_HALF ** -0.5),
        "peer_w_up": nrm(ks[19], (L, PEER_EXPERTS, D), D ** -0.5),
        "peer_w_down": nrm(ks[20], (L, PEER_EXPERTS, D), PEER_HEADS ** -0.5),
    }


def reference(x, c, norm1_g, norm2_g, w_ada, b_ada, w_in, conv_dw, conv_db, conv_ln_g, conv_ln_b,
              w_conv_out, b_conv_out, q_norm_g, k_norm_g, w_attn_o, w_out,
              peer_w_q, peer_sub_keys, peer_w_up, peer_w_down):
    B, S, D = x.shape
    for l in range(DEPTH):
        mod = (jax.nn.silu(c) @ w_ada[l] + b_ada[l]).reshape(B, 6, D)
        shift1, scale1, gate1 = mod[:, 0, None], mod[:, 1, None], mod[:, 2, None]
        shift2, scale2, gate2 = mod[:, 3, None], mod[:, 4, None], mod[:, 5, None]
        h = rmsnorm(x, norm1_g[l]) * (1.0 + scale1) + shift1
        x = x + gate1 * token_mixer(h, w_in[l], conv_dw[l], conv_db[l], conv_ln_g[l], conv_ln_b[l],
                                    w_conv_out[l], b_conv_out[l], q_norm_g[l], k_norm_g[l],
                                    w_attn_o[l], w_out[l])
        h = rmsnorm(x, norm2_g[l]) * (1.0 + scale2) + shift2
        x = x + gate2 * peer(h, peer_w_q[l], peer_sub_keys[l], peer_w_up[l], peer_w_down[l])
    return x
```

```python
import functools

import jax
import jax.numpy as jnp
from jax import lax
from jax.experimental import pallas as pl
from jax.experimental.pallas import tpu as pltpu

F32 = jnp.float32
BF16 = jnp.bfloat16

EPS = 1e-6
CONV_K = 31
HEAD_DIM = 128
HEADS_PER_GROUP = 4
GROUP_WIDTH = HEADS_PER_GROUP * HEAD_DIM
DILATIONS = (1, 4, 16)
BAND = 128
PEER_HEADS = 8
PEER_NKEYS = 128
PEER_TOPK = 16
HALO = 32

VMEM_LIMIT = 56 * 1024 * 1024


def _params(*sem):
    return pltpu.CompilerParams(dimension_semantics=sem, vmem_limit_bytes=VMEM_LIMIT)


def _ada_kernel(ct_ref, w_ref, b_ref, o_ref):
    d, tn = w_ref.shape
    nb = ct_ref.shape[1]
    rows = 8

    def body(i, accs):
        r0 = pl.multiple_of(i * rows, rows)
        w8 = w_ref[pl.ds(r0, rows), :]
        c8 = ct_ref[pl.ds(r0, rows), :]
        c8 = c8 * jax.nn.sigmoid(c8)
        return tuple(a + w8 * c8[:, b:b + 1] for b, a in enumerate(accs))

    accs = lax.fori_loop(0, d // rows, body,
                         tuple(jnp.zeros((rows, tn), F32) for _ in range(nb)), unroll=8)
    out = jnp.concatenate([jnp.sum(a, axis=0, keepdims=True) for a in accs], axis=0)
    o_ref[...] = out + b_ref[...]


def _ada(c, w_ada, b_ada):
    nb, d = c.shape
    n = w_ada.shape[1]
    tn = 1024
    return pl.pallas_call(
        _ada_kernel,
        grid=(n // tn,),
        in_specs=[pl.BlockSpec((d, nb), lambda j: (0, 0)),
                  pl.BlockSpec((d, tn), lambda j: (0, j)),
                  pl.BlockSpec((1, tn), lambda j: (0, j))],
        out_specs=pl.BlockSpec((nb, tn), lambda j: (0, j)),
        out_shape=jax.ShapeDtypeStruct((nb, n), F32),
        compiler_params=_params("arbitrary"),
        name="ada_mod",
    )(c.T, w_ada, b_ada.reshape(1, n))


def _modulated_rmsnorm(x, g, shift, scale):
    y = x * lax.rsqrt(jnp.mean(x * x, axis=-1, keepdims=True) + EPS) * g
    return y * (1.0 + scale) + shift


def _prenorm_kernel(x_ref, g_ref, mod_ref, o_ref):
    h = _modulated_rmsnorm(x_ref[0], g_ref[...], mod_ref[0, 0:1, :], mod_ref[0, 1:2, :])
    o_ref[0] = h.astype(o_ref.dtype)


def _prenorm(x, g, mod3):
    b, s, d = x.shape
    ts = 512
    return pl.pallas_call(
        _prenorm_kernel,
        grid=(b, s // ts),
        in_specs=[pl.BlockSpec((1, ts, d), lambda bi, i: (bi, i, 0)),
                  pl.BlockSpec((1, d), lambda bi, i: (0, 0)),
                  pl.BlockSpec((1, 6, d), lambda bi, i: (bi, 0, 0))],
        out_specs=pl.BlockSpec((1, ts, d), lambda bi, i: (bi, i, 0)),
        out_shape=jax.ShapeDtypeStruct((b, s, d), BF16),
        compiler_params=_params("parallel", "parallel"),
        name="prenorm",
    )(x, g.reshape(1, d), mod3)


def _matmul_kernel(a_ref, b_ref, o_ref):
    o_ref[...] = jnp.dot(a_ref[...], b_ref[...], preferred_element_type=F32).astype(o_ref.dtype)


def _in_proj(a, b, tm, tn, lo, mid):
    m, k = a.shape
    n = b.shape[1]
    nt = n // tn

    def out_col(j):
        return jnp.where(j < lo, j, jnp.where(j < mid, j + (nt - mid), j - (mid - lo)))

    return pl.pallas_call(
        _matmul_kernel,
        grid=(nt, m // tm),
        in_specs=[pl.BlockSpec((tm, k), lambda j, i: (i, 0)),
                  pl.BlockSpec((k, tn), lambda j, i: (0, j))],
        out_specs=pl.BlockSpec((tm, tn), lambda j, i: (i, out_col(j))),
        out_shape=jax.ShapeDtypeStruct((m, n), BF16),
        compiler_params=_params("parallel", "parallel"),
        name="in_proj",
    )(a, b)


def _conv_kernel(ac_ref, gc_ref, ap_ref, gp_ref, dw_ref, db_ref, lg_ref, lb_ref, o_ref, buf_ref):
    ts = ac_ref.shape[1]
    i = pl.program_id(1)

    def glu(a, g):
        return a.astype(F32) * jax.nn.sigmoid(g.astype(F32))

    halo = glu(ap_ref[0], gp_ref[0])
    buf_ref[0:HALO, :] = jnp.where(i > 0, halo, 0.0)
    buf_ref[HALO:, :] = glu(ac_ref[0], gc_ref[0])
    rows = 32
    first = HALO - (CONV_K - 1)
    for r in range(ts // rows):
        acc = jnp.zeros((rows, ac_ref.shape[2]), F32)
        for j in range(CONV_K):
            lo = r * rows + first + j
            acc = acc + dw_ref[j:j + 1, :] * buf_ref[lo:lo + rows, :]
        acc = acc + db_ref[...]
        mu = jnp.mean(acc, axis=-1, keepdims=True)
        cen = acc - mu
        var = jnp.mean(cen * cen, axis=-1, keepdims=True)
        y = cen * lax.rsqrt(var + EPS) * lg_ref[...] + lb_ref[...]
        o_ref[0, r * rows:(r + 1) * rows, :] = (y * jax.nn.sigmoid(y)).astype(o_ref.dtype)


def _conv_branch(z3, conv_dw, conv_db, ln_g, ln_b):
    b, s, _ = z3.shape
    ch = conv_dw.shape[1]
    ts = 256
    per = ts // HALO
    prev = lambda col: (lambda bi, i: (bi, jnp.maximum(i * per - 1, 0), col))
    vec = pl.BlockSpec((1, ch), lambda bi, i: (0, 0))
    return pl.pallas_call(
        _conv_kernel,
        grid=(b, s // ts),
        in_specs=[pl.BlockSpec((1, ts, ch), lambda bi, i: (bi, i, 0)),
                  pl.BlockSpec((1, ts, ch), lambda bi, i: (bi, i, 1)),
                  pl.BlockSpec((1, HALO, ch), prev(0)),
                  pl.BlockSpec((1, HALO, ch), prev(1)),
                  pl.BlockSpec((CONV_K, ch), lambda bi, i: (0, 0)),
                  vec, vec, vec],
        out_specs=pl.BlockSpec((1, ts, ch), lambda bi, i: (bi, i, 0)),
        out_shape=jax.ShapeDtypeStruct((b, s, ch), BF16),
        scratch_shapes=[pltpu.VMEM((ts + HALO, ch), F32)],
        compiler_params=_params("parallel", "parallel"),
        name="conv_branch",
    )(z3, z3, z3, z3, conv_dw, conv_db.reshape(1, ch), ln_g.reshape(1, ch), ln_b.reshape(1, ch))


def _attn_kernel(q_ref, kp_ref, kc_ref, vp_ref, vc_ref, qg_ref, kg_ref, o_ref, l_ref):
    n = pl.program_id(2)
    qi = lax.broadcasted_iota(jnp.int32, (BAND, BAND), 0)
    ki = lax.broadcasted_iota(jnp.int32, (BAND, BAND), 1)
    mask_prev = ki >= qi + jnp.where(n > 0, 0, 2 * BAND)
    mask_cur = ki <= qi

    def headnorm(t, g):
        t = t.astype(F32)
        return t * lax.rsqrt(jnp.mean(t * t, axis=-1, keepdims=True) + EPS) * g

    nt = (((1,), (1,)), ((), ()))
    for j in range(HEADS_PER_GROUP):
        sl = slice(j * HEAD_DIM, (j + 1) * HEAD_DIM)
        q = (headnorm(q_ref[0, :, sl], qg_ref[:, sl]) * (HEAD_DIM ** -0.5)).astype(BF16)
        kp = headnorm(kp_ref[0, :, sl], kg_ref[:, sl]).astype(BF16)
        kc = headnorm(kc_ref[0, :, sl], kg_ref[:, sl]).astype(BF16)
        sp = jnp.where(mask_prev, lax.dot_general(q, kp, nt, preferred_element_type=F32), -jnp.inf)
        sc = jnp.where(mask_cur, lax.dot_general(q, kc, nt, preferred_element_type=F32), -jnp.inf)
        m = jnp.maximum(jnp.max(sp, axis=-1, keepdims=True), jnp.max(sc, axis=-1, keepdims=True))
        pp = jnp.exp(sp - m)
        pc = jnp.exp(sc - m)
        den = jnp.sum(pp, axis=-1, keepdims=True) + jnp.sum(pc, axis=-1, keepdims=True)
        o = (jnp.dot(pp.astype(BF16), vp_ref[0, :, sl], preferred_element_type=F32)
             + jnp.dot(pc.astype(BF16), vc_ref[0, :, sl], preferred_element_type=F32))
        o_ref[0, :, sl] = (o / den).astype(o_ref.dtype)
        l_ref[0, :, sl] = jnp.broadcast_to(m + jnp.log(den), (BAND, HEAD_DIM))


def _attn_group(z3, q_g, k_g, gi, dil, in_cols, off_q, off_k, off_v):
    b, s, _ = z3.shape
    l = s // dil
    nb = l // BAND
    zv = z3.reshape(b, l, dil * in_cols)
    per = in_cols // GROUP_WIDTH
    cq, ck, cv = (off // GROUP_WIDTH + gi for off in (off_q, off_k, off_v))
    cur = lambda c: (lambda bi, r, n: (bi, n, r * per + c))
    prv = lambda c: (lambda bi, r, n: (bi, jnp.maximum(n - 1, 0), r * per + c))
    blk = (1, BAND, GROUP_WIDTH)
    gain = pl.BlockSpec((1, GROUP_WIDTH), lambda bi, r, n: (0, 0))
    out_spec = pl.BlockSpec(blk, lambda bi, r, n: (bi, n, r))
    o, lse = pl.pallas_call(
        _attn_kernel,
        grid=(b, dil, nb),
        in_specs=[pl.BlockSpec(blk, cur(cq)), pl.BlockSpec(blk, prv(ck)), pl.BlockSpec(blk, cur(ck)),
                  pl.BlockSpec(blk, prv(cv)), pl.BlockSpec(blk, cur(cv)), gain, gain],
        out_specs=[out_spec, out_spec],
        out_shape=[jax.ShapeDtypeStruct((b, l, dil * GROUP_WIDTH), BF16),
                   jax.ShapeDtypeStruct((b, l, dil * GROUP_WIDTH), F32)],
        compiler_params=_params("parallel", "parallel", "arbitrary"),
        name=f"dilated_attn_{dil}",
    )(zv, zv, zv, zv, zv, q_g, k_g)
    return o.reshape(b, s, GROUP_WIDTH), lse.reshape(b, s, GROUP_WIDTH)


def _post_kernel(u_ref, o0_ref, o1_ref, o2_ref, l0_ref, l1_ref, l2_ref, ga_ref, gb_ref, x_ref,
                 mod_ref, n2_ref, wc_ref, bc_ref, wa_ref, wo_ref, x1_ref, h2_ref):
    yc = jnp.dot(u_ref[0], wc_ref[...], preferred_element_type=F32) + bc_ref[...]
    l0, l1, l2 = l0_ref[0], l1_ref[0], l2_ref[0]
    lm = jnp.maximum(jnp.maximum(l0, l1), l2)
    e0, e1, e2 = jnp.exp(l0 - lm), jnp.exp(l1 - lm), jnp.exp(l2 - lm)
    o = (e0 * o0_ref[0].astype(F32) + e1 * o1_ref[0].astype(F32) + e2 * o2_ref[0].astype(F32)) / (e0 + e1 + e2)
    ya = jnp.dot(o.astype(BF16), wa_ref[...], preferred_element_type=F32)
    merged = (jax.nn.sigmoid(ga_ref[0].astype(F32)) * yc + jax.nn.sigmoid(gb_ref[0].astype(F32)) * ya)
    out = jnp.dot(merged.astype(BF16), wo_ref[...], preferred_element_type=F32)
    x1 = x_ref[0] + mod_ref[0, 2:3, :] * out
    x1_ref[0] = x1
    h2 = _modulated_rmsnorm(x1, n2_ref[...], mod_ref[0, 3:4, :], mod_ref[0, 4:5, :])
    h2_ref[0] = h2.astype(h2_ref.dtype)


def _post(u2, os_, ls_, z3, off_g, x, mod3, norm2_g, wc, bc, wa, wo):
    b, s, d = x.shape
    ch = u2.shape[2]
    tm = 256
    gcol = off_g // d
    tok = lambda w: pl.BlockSpec((1, tm, w), lambda bi, i: (bi, i, 0))
    full = lambda a: pl.BlockSpec(a.shape, lambda bi, i: (0,) * a.ndim, pipeline_mode=pl.Buffered(1))
    n2 = norm2_g.reshape(1, d)
    bc2 = bc.reshape(1, d)
    return pl.pallas_call(
        _post_kernel,
        grid=(b, s // tm),
        in_specs=[tok(ch)] + [tok(GROUP_WIDTH)] * 6
                 + [pl.BlockSpec((1, tm, d), lambda bi, i: (bi, i, gcol)),
                    pl.BlockSpec((1, tm, d), lambda bi, i: (bi, i, gcol + 1)),
                    tok(d),
                    pl.BlockSpec((1, 6, d), lambda bi, i: (bi, 0, 0)),
                    full(n2), full(wc), full(bc2), full(wa), full(wo)],
        out_specs=[tok(d), tok(d)],
        out_shape=[jax.ShapeDtypeStruct((b, s, d), F32), jax.ShapeDtypeStruct((b, s, d), BF16)],
        compiler_params=_params("parallel", "parallel"),
        name="merge_out_proj",
    )(u2, *os_, *ls_, z3, z3, x, mod3, n2, wc, bc2, wa, wo)


def _peer_scores_kernel(h_ref, wq_ref, keys_ref, s_ref):
    qp = jnp.dot(h_ref[...], wq_ref[...], preferred_element_type=F32).astype(BF16)
    nt = (((1,), (1,)), ((), ()))
    half = keys_ref.shape[2]
    for hp in range(keys_ref.shape[0]):
        s_ref[hp] = lax.dot_general(keys_ref[hp], qp[:, hp * half:(hp + 1) * half], nt,
                                    preferred_element_type=F32)


def _peer_scores(h2, wq, keys):
    t, d = h2.shape
    nhp, nk, half = keys.shape
    tm = 256
    return pl.pallas_call(
        _peer_scores_kernel,
        grid=(t // tm,),
        in_specs=[pl.BlockSpec((tm, d), lambda i: (i, 0)),
                  pl.BlockSpec(wq.shape, lambda i: (0, 0), pipeline_mode=pl.Buffered(1)),
                  pl.BlockSpec(keys.shape, lambda i: (0, 0, 0), pipeline_mode=pl.Buffered(1))],
        out_specs=pl.BlockSpec((nhp, nk, tm), lambda i: (0, 0, i)),
        out_shape=jax.ShapeDtypeStruct((nhp, nk, t), F32),
        compiler_params=_params("parallel"),
        name="peer_scores",
    )(h2, wq, keys)


def _top16(s):
    key = lax.broadcasted_iota(jnp.int32, s.shape, 0)
    slot = lax.broadcasted_iota(jnp.int32, (PEER_TOPK, s.shape[1]), 0)
    rank = jnp.full(s.shape, PEER_TOPK, jnp.int32)
    vals = jnp.zeros((PEER_TOPK, s.shape[1]), F32)
    for k in range(PEER_TOPK):
        m = jnp.max(s, axis=0, keepdims=True)
        first = jnp.min(jnp.where(s == m, key, PEER_NKEYS), axis=0, keepdims=True)
        sel = key == first
        rank = jnp.where(sel, k, rank)
        s = jnp.where(sel, -jnp.inf, s)
        vals = jnp.where(slot == k, m, vals)
    return vals, rank


def _staircase_lengths(v1, v2):
    tl = v1.shape[1]
    j8 = lax.broadcasted_iota(jnp.int32, (8, tl), 0)
    j16 = lax.broadcasted_iota(jnp.int32, (PEER_TOPK, tl), 0)
    cands = [v1[0:1, :] + v2]
    flats = [j16]
    for i in range(1, 8):
        cands.append(jnp.where(j8 < PEER_TOPK // (i + 1), v1[i:i + 1, :] + v2[0:8, :], -jnp.inf))
        flats.append(j8 + i * PEER_TOPK)
    cands.append(v1[8:16, :] + v2[0:1, :])
    flats.append((j8 + 8) * PEER_TOPK)
    cand = jnp.concatenate(cands, axis=0)
    flat = jnp.concatenate(flats, axis=0)
    big = PEER_TOPK * PEER_TOPK
    picked = jnp.zeros(cand.shape, F32)
    z = jnp.zeros((1, tl), F32)
    top0 = None
    for k in range(PEER_TOPK):
        m = jnp.max(cand, axis=0, keepdims=True)
        first = jnp.min(jnp.where(cand == m, flat, big), axis=0, keepdims=True)
        sel = flat == first
        picked = jnp.where(sel, 1.0, picked)
        cand = jnp.where(sel, -jnp.inf, cand)
        if k == 0:
            top0 = m
        z = z + jnp.exp(m - top0)
    lens_lo = jnp.zeros((8, tl), F32)
    lens_lo = jnp.where(j8 == 0, jnp.sum(picked[0:16], axis=0, keepdims=True), lens_lo)
    for i in range(1, 8):
        lens_lo = jnp.where(j8 == i, jnp.sum(picked[8 + 8 * i:16 + 8 * i], axis=0, keepdims=True), lens_lo)
    return jnp.concatenate([lens_lo, picked[72:80]], axis=0), z


def _peer_select_kernel(s_ref, e1_ref, len_ref, r2_ref, e2_ref):
    def per_head(h, carry):
        s1 = s_ref[2 * h]
        s2 = s_ref[2 * h + 1]
        v1, rank1 = _top16(s1)
        v2, rank2 = _top16(s2)
        lens, z = _staircase_lengths(v1, v2)
        sel_len = jnp.zeros(s1.shape, F32)
        for i in range(PEER_TOPK):
            sel_len = jnp.where(rank1 == i, lens[i:i + 1, :], sel_len)
        e1_ref[h] = jnp.exp(s1 - v1[0:1, :])
        len_ref[h] = sel_len
        r2_ref[h] = rank2.astype(F32).astype(r2_ref.dtype)
        e2_ref[h] = (jnp.exp(s2 - v2[0:1, :]) / z).astype(e2_ref.dtype)
        return carry

    lax.fori_loop(0, PEER_HEADS, per_head, 0)


def _peer_select(scores_t):
    nhp, nk, t = scores_t.shape
    tl = 128
    out_spec = pl.BlockSpec((PEER_HEADS, nk, tl), lambda i: (0, 0, i))
    shp = lambda dt: jax.ShapeDtypeStruct((PEER_HEADS, nk, t), dt)
    return pl.pallas_call(
        _peer_select_kernel,
        grid=(t // tl,),
        in_specs=[pl.BlockSpec((nhp, nk, tl), lambda i: (0, 0, i))],
        out_specs=[out_spec] * 4,
        out_shape=[shp(F32), shp(F32), shp(BF16), shp(BF16)],
        compiler_params=_params("parallel"),
        name="peer_select",
    )(scores_t)


def _peer_dense_kernel(h_ref, wu_ref, wdt_ref, e1_ref, len_ref, r2_ref, e2_ref, x1_ref, mod_ref,
                       o_ref, acc_ref):
    j = pl.program_id(1)

    @pl.when(j == 0)
    def _():
        acc_ref[...] = jnp.zeros_like(acc_ref)

    nt = (((1,), (1,)), ((), ()))
    pre = lax.dot_general(wu_ref[...], h_ref[...], nt, preferred_element_type=F32)
    act = (0.5 * pre * (1.0 + lax.erf(pre * (0.5 ** 0.5)))).astype(BF16)
    nk = r2_ref.shape[1]
    parts = []
    for c in range(wu_ref.shape[0] // nk):
        gate = jnp.zeros((nk, h_ref.shape[0]), BF16)
        for h in range(PEER_HEADS):
            sel_len = len_ref[h, c:c + 1, :].astype(BF16)
            e1 = e1_ref[h, c:c + 1, :].astype(BF16)
            gate = gate + jnp.where(r2_ref[h] < sel_len, e1 * e2_ref[h], jnp.zeros((), BF16))
        parts.append(act[c * nk:(c + 1) * nk, :] * gate)
    p = jnp.concatenate(parts, axis=0)
    acc_ref[...] += jnp.dot(wdt_ref[...], p, preferred_element_type=F32)

    @pl.when(j == pl.num_programs(1) - 1)
    def _():
        o_ref[...] = x1_ref[...] + mod_ref[0, 5:6, :] * acc_ref[...].T


def _peer_dense(h2, wu, wdt, e1t, lent, r2t, e2t, x1, mod3):
    t, d = h2.shape
    ne = wu.shape[0]
    nk = r2t.shape[1]
    tl = 512
    te = 8 * nk
    small = pl.BlockSpec((PEER_HEADS, 8, tl), lambda i, j: (0, j, i))
    table = pl.BlockSpec((PEER_HEADS, nk, tl), lambda i, j: (0, 0, i))
    per_b = t // mod3.shape[0] // tl
    once = pl.Buffered(1)
    return pl.pallas_call(
        _peer_dense_kernel,
        grid=(t // tl, ne // te),
        in_specs=[pl.BlockSpec((tl, d), lambda i, j: (i, 0), pipeline_mode=once),
                  pl.BlockSpec((te, d), lambda i, j: (j, 0)),
                  pl.BlockSpec((d, te), lambda i, j: (0, j)),
                  small, small, table, table,
                  pl.BlockSpec((tl, d), lambda i, j: (i, 0), pipeline_mode=once),
                  pl.BlockSpec((1, 6, d), lambda i, j: (i // per_b, 0, 0))],
        out_specs=pl.BlockSpec((tl, d), lambda i, j: (i, 0)),
        out_shape=jax.ShapeDtypeStruct((t, d), F32),
        scratch_shapes=[pltpu.VMEM((d, tl), F32)],
        compiler_params=_params("parallel", "arbitrary"),
        name="peer_dense",
    )(h2, wu, wdt, e1t, lent, r2t, e2t, x1, mod3)


def kernel(x, c, norm1_g, norm2_g, w_ada, b_ada, w_in, conv_dw, conv_db, conv_ln_g, conv_ln_b,
           w_conv_out, b_conv_out, q_norm_g, k_norm_g, w_attn_o, w_out,
           peer_w_q, peer_sub_keys, peer_w_up, peer_w_down):
    b, s, d = x.shape
    t = b * s
    depth = w_ada.shape[0]
    conv_ch = conv_dw.shape[2]
    attn_w = q_norm_g.shape[1] * q_norm_g.shape[2]
    in_cols = w_in.shape[2]
    glu_w = 2 * conv_ch
    gate_w = 2 * d
    off_g = glu_w
    off_q = glu_w + gate_w
    off_k = off_q + attn_w
    off_v = off_k + attn_w
    tn = GROUP_WIDTH
    for l in range(depth):
        mod3 = _ada(c, w_ada[l], b_ada[l]).reshape(b, 6, d)
        h1 = _prenorm(x, norm1_g[l], mod3)
        z = _in_proj(h1.reshape(t, d), w_in[l].astype(BF16), 1024, tn,
                     glu_w // tn, (glu_w + 3 * attn_w) // tn)
        z3 = z.reshape(b, s, in_cols)
        u2 = _conv_branch(z3, conv_dw[l], conv_db[l], conv_ln_g[l], conv_ln_b[l])
        q_g = q_norm_g[l].reshape(len(DILATIONS), 1, GROUP_WIDTH)
        k_g = k_norm_g[l].reshape(len(DILATIONS), 1, GROUP_WIDTH)
        outs, lses = [], []
        for gi, dil in enumerate(DILATIONS):
            o, lse = _attn_group(z3, q_g[gi], k_g[gi], gi, dil, in_cols, off_q, off_k, off_v)
            outs.append(o)
            lses.append(lse)
        x1, h2 = _post(u2, outs, lses, z3, off_g, x, mod3, norm2_g[l],
                       w_conv_out[l].astype(BF16), b_conv_out[l], w_attn_o[l].astype(BF16),
                       w_out[l].astype(BF16))
        h2 = h2.reshape(t, d)
        keys = peer_sub_keys[l].reshape(2 * PEER_HEADS, PEER_NKEYS, -1).astype(BF16)
        scores_t = _peer_scores(h2, peer_w_q[l].astype(BF16), keys)
        e1t, lent, r2t, e2t = _peer_select(scores_t)
        x = _peer_dense(h2, peer_w_up[l].astype(BF16), peer_w_down[l].T.astype(BF16),
                        e1t, lent, r2t, e2t, x1.reshape(t, d), mod3).reshape(b, s, d)
    return x
```

```python
import functools

import jax
import jax.numpy as jnp
from jax import lax
from jax.experimental import pallas as pl
from jax.experimental.pallas import tpu as pltpu

F32 = jnp.float32
BF16 = jnp.bfloat16

EPS = 1e-6
CONV_K = 31
HEAD_DIM = 128
HEADS_PER_GROUP = 4
GROUP_WIDTH = HEADS_PER_GROUP * HEAD_DIM
DILATIONS = (1, 4, 16)
BAND = 128
PEER_HEADS = 8
PEER_NKEYS = 128
PEER_TOPK = 16
HALO = 32
LANES = 128

VMEM_LIMIT = 56 * 1024 * 1024


def _params(*sem):
    return pltpu.CompilerParams(dimension_semantics=sem, vmem_limit_bytes=VMEM_LIMIT)


def _ada_kernel(ct_ref, w_ref, b_ref, o_ref):
    d, tn = w_ref.shape
    nb = ct_ref.shape[1]
    rows = 8

    def body(i, accs):
        r0 = pl.multiple_of(i * rows, rows)
        w8 = w_ref[pl.ds(r0, rows), :]
        c8 = ct_ref[pl.ds(r0, rows), :]
        c8 = c8 * jax.nn.sigmoid(c8)
        return tuple(a + w8 * c8[:, b:b + 1] for b, a in enumerate(accs))

    accs = lax.fori_loop(0, d // rows, body,
                         tuple(jnp.zeros((rows, tn), F32) for _ in range(nb)), unroll=8)
    out = jnp.concatenate([jnp.sum(a, axis=0, keepdims=True) for a in accs], axis=0)
    o_ref[...] = out + b_ref[...]


def _ada(c, w_ada, b_ada):
    nb, d = c.shape
    n = w_ada.shape[1]
    tn = 1024
    return pl.pallas_call(
        _ada_kernel,
        grid=(n // tn,),
        in_specs=[pl.BlockSpec((d, nb), lambda j: (0, 0)),
                  pl.BlockSpec((d, tn), lambda j: (0, j)),
                  pl.BlockSpec((1, tn), lambda j: (0, j))],
        out_specs=pl.BlockSpec((nb, tn), lambda j: (0, j)),
        out_shape=jax.ShapeDtypeStruct((nb, n), F32),
        compiler_params=_params("arbitrary"),
        name="ada_mod",
    )(c.T, w_ada, b_ada.reshape(1, n))


def _modulated_rmsnorm(x, g, shift, scale):
    y = x * lax.rsqrt(jnp.mean(x * x, axis=-1, keepdims=True) + EPS) * g
    return y * (1.0 + scale) + shift


def _prenorm_kernel(x_ref, g_ref, mod_ref, o_ref, *rest):
    perm_refs, h_ref = rest[:-1], rest[-1]
    ts = x_ref.shape[1]
    h = _modulated_rmsnorm(x_ref[0], g_ref[...], mod_ref[0, 0:1, :], mod_ref[0, 1:2, :])
    o_ref[0] = h.astype(o_ref.dtype)
    for c in range(h_ref.shape[0]):
        h_ref[c] = h[:, c * LANES:(c + 1) * LANES]
    for p_ref in perm_refs:
        dil = p_ref.shape[1]
        for r in range(dil):
            for c in range(h_ref.shape[0]):
                rows = h_ref[c, pl.ds(r, ts // dil, stride=dil), :]
                p_ref[0, r, :, c * LANES:(c + 1) * LANES] = rows.astype(p_ref.dtype)


def _prenorm(x, g, mod3):
    b, s, d = x.shape
    ts = 512
    dils = [dil for dil in DILATIONS if dil > 1]
    nat = pl.BlockSpec((1, ts, d), lambda bi, i: (bi, i, 0))
    return pl.pallas_call(
        _prenorm_kernel,
        grid=(b, s // ts),
        in_specs=[nat,
                  pl.BlockSpec((1, d), lambda bi, i: (0, 0)),
                  pl.BlockSpec((1, 6, d), lambda bi, i: (bi, 0, 0))],
        out_specs=[nat] + [pl.BlockSpec((1, dil, ts // dil, d), lambda bi, i: (bi, 0, i, 0)) for dil in dils],
        out_shape=[jax.ShapeDtypeStruct((b, s, d), BF16)]
                  + [jax.ShapeDtypeStruct((b, dil, s // dil, d), BF16) for dil in dils],
        scratch_shapes=[pltpu.VMEM((d // LANES, ts, LANES), F32)],
        compiler_params=_params("parallel", "parallel"),
        name="prenorm",
    )(x, g.reshape(1, d), mod3)


def _matmul_kernel(cols_ref, a_ref, b_ref, o_ref):
    del cols_ref
    o_ref[...] = jnp.dot(a_ref[...], b_ref[...], preferred_element_type=F32).astype(o_ref.dtype)


def _in_proj(a, b, tm, tn, col_tiles, name):
    m, k = a.shape
    nt = len(col_tiles)
    return pl.pallas_call(
        _matmul_kernel,
        grid_spec=pltpu.PrefetchScalarGridSpec(
            num_scalar_prefetch=1,
            grid=(nt, m // tm),
            in_specs=[pl.BlockSpec((tm, k), lambda j, i, cols: (i, 0)),
                      pl.BlockSpec((k, tn), lambda j, i, cols: (0, cols[j]))],
            out_specs=pl.BlockSpec((tm, tn), lambda j, i, cols: (i, j))),
        out_shape=jax.ShapeDtypeStruct((m, nt * tn), BF16),
        compiler_params=_params("parallel", "parallel"),
        name=name,
    )(jnp.asarray(col_tiles, jnp.int32), a, b)


def _conv_kernel(ac_ref, gc_ref, ap_ref, gp_ref, dw_ref, db_ref, lg_ref, lb_ref, o_ref, buf_ref):
    ts = ac_ref.shape[1]
    i = pl.program_id(1)

    def glu(a, g):
        return a.astype(F32) * jax.nn.sigmoid(g.astype(F32))

    halo = glu(ap_ref[0], gp_ref[0])
    buf_ref[0:HALO, :] = jnp.where(i > 0, halo, 0.0)
    buf_ref[HALO:, :] = glu(ac_ref[0], gc_ref[0])
    rows = 32
    first = HALO - (CONV_K - 1)
    for r in range(ts // rows):
        acc = jnp.zeros((rows, ac_ref.shape[2]), F32)
        for j in range(CONV_K):
            lo = r * rows + first + j
            acc = acc + dw_ref[j:j + 1, :] * buf_ref[lo:lo + rows, :]
        acc = acc + db_ref[...]
        mu = jnp.mean(acc, axis=-1, keepdims=True)
        cen = acc - mu
        var = jnp.mean(cen * cen, axis=-1, keepdims=True)
        y = cen * lax.rsqrt(var + EPS) * lg_ref[...] + lb_ref[...]
        o_ref[0, r * rows:(r + 1) * rows, :] = (y * jax.nn.sigmoid(y)).astype(o_ref.dtype)


def _conv_branch(z3, conv_dw, conv_db, ln_g, ln_b):
    b, s, _ = z3.shape
    ch = conv_dw.shape[1]
    ts = 256
    per = ts // HALO
    prev = lambda col: (lambda bi, i: (bi, jnp.maximum(i * per - 1, 0), col))
    vec = pl.BlockSpec((1, ch), lambda bi, i: (0, 0))
    return pl.pallas_call(
        _conv_kernel,
        grid=(b, s // ts),
        in_specs=[pl.BlockSpec((1, ts, ch), lambda bi, i: (bi, i, 0)),
                  pl.BlockSpec((1, ts, ch), lambda bi, i: (bi, i, 1)),
                  pl.BlockSpec((1, HALO, ch), prev(0)),
                  pl.BlockSpec((1, HALO, ch), prev(1)),
                  pl.BlockSpec((CONV_K, ch), lambda bi, i: (0, 0)),
                  vec, vec, vec],
        out_specs=pl.BlockSpec((1, ts, ch), lambda bi, i: (bi, i, 0)),
        out_shape=jax.ShapeDtypeStruct((b, s, ch), BF16),
        scratch_shapes=[pltpu.VMEM((ts + HALO, ch), F32)],
        compiler_params=_params("parallel", "parallel"),
        name="conv_branch",
    )(z3, z3, z3, z3, conv_dw, conv_db.reshape(1, ch), ln_g.reshape(1, ch), ln_b.reshape(1, ch))


def _attn_kernel(q_ref, kp_ref, kc_ref, vp_ref, vc_ref, qg_ref, kg_ref, o_ref, l_ref):
    n = pl.program_id(1)
    qi = lax.broadcasted_iota(jnp.int32, (BAND, BAND), 0)
    ki = lax.broadcasted_iota(jnp.int32, (BAND, BAND), 1)
    mask_prev = ki >= qi + jnp.where(n > 0, 0, 2 * BAND)
    mask_cur = ki <= qi

    def headnorm(t, g):
        t = t.astype(F32)
        return t * lax.rsqrt(jnp.mean(t * t, axis=-1, keepdims=True) + EPS) * g

    nt = (((1,), (1,)), ((), ()))
    for j in range(HEADS_PER_GROUP):
        sl = slice(j * HEAD_DIM, (j + 1) * HEAD_DIM)
        q = (headnorm(q_ref[0, :, sl], qg_ref[:, sl]) * (HEAD_DIM ** -0.5)).astype(BF16)
        kp = headnorm(kp_ref[0, :, sl], kg_ref[:, sl]).astype(BF16)
        kc = headnorm(kc_ref[0, :, sl], kg_ref[:, sl]).astype(BF16)
        sp = jnp.where(mask_prev, lax.dot_general(q, kp, nt, preferred_element_type=F32), -jnp.inf)
        sc = jnp.where(mask_cur, lax.dot_general(q, kc, nt, preferred_element_type=F32), -jnp.inf)
        m = jnp.maximum(jnp.max(sp, axis=-1, keepdims=True), jnp.max(sc, axis=-1, keepdims=True))
        pp = jnp.exp(sp - m)
        pc = jnp.exp(sc - m)
        den = jnp.sum(pp, axis=-1, keepdims=True) + jnp.sum(pc, axis=-1, keepdims=True)
        o = (jnp.dot(pp.astype(BF16), vp_ref[0, :, sl], preferred_element_type=F32)
             + jnp.dot(pc.astype(BF16), vc_ref[0, :, sl], preferred_element_type=F32))
        o_ref[0, :, sl] = (o / den).astype(o_ref.dtype)
        l_ref[0, :, sl] = jnp.broadcast_to(m + jnp.log(den), (BAND, HEAD_DIM))


def _attn_group(zg, q_g, k_g, dil, cq, ck, cv):
    nseq, l, _ = zg.shape
    nb = l // BAND
    cur = lambda c: (lambda si, n: (si, n, c))
    prv = lambda c: (lambda si, n: (si, jnp.maximum(n - 1, 0), c))
    blk = (1, BAND, GROUP_WIDTH)
    gain = pl.BlockSpec((1, GROUP_WIDTH), lambda si, n: (0, 0))
    out_spec = pl.BlockSpec(blk, lambda si, n: (si, n, 0))
    return pl.pallas_call(
        _attn_kernel,
        grid=(nseq, nb),
        in_specs=[pl.BlockSpec(blk, cur(cq)), pl.BlockSpec(blk, prv(ck)), pl.BlockSpec(blk, cur(ck)),
                  pl.BlockSpec(blk, prv(cv)), pl.BlockSpec(blk, cur(cv)), gain, gain],
        out_specs=[out_spec, out_spec],
        out_shape=[jax.ShapeDtypeStruct((nseq, l, GROUP_WIDTH), BF16),
                   jax.ShapeDtypeStruct((nseq, l, GROUP_WIDTH), F32)],
        compiler_params=_params("parallel", "arbitrary"),
        name=f"dilated_attn_{dil}",
    )(zg, zg, zg, zg, zg, q_g, k_g)


def _post_kernel(u_ref, o0_ref, o1_ref, o2_ref, l0_ref, l1_ref, l2_ref, ga_ref, gb_ref, x_ref,
                 mod_ref, n2_ref, wc_ref, bc_ref, wa_ref, wo_ref, x1_ref, h2_ref,
                 so1_ref, sl1_ref, so2_ref, sl2_ref):
    yc = jnp.dot(u_ref[0], wc_ref[...], preferred_element_type=F32) + bc_ref[...]

    def sequence_order(src_ref, dst_ref):
        dil, rows = src_ref.shape[1], src_ref.shape[2]
        nch = dst_ref.shape[0]
        for r in range(dil):
            src = src_ref[0, r].astype(F32)
            for c in range(nch):
                dst_ref[c, pl.ds(r, rows, stride=dil), :] = src[:, c * LANES:(c + 1) * LANES]
        return jnp.concatenate([dst_ref[c] for c in range(nch)], axis=1)

    o0, l0 = o0_ref[0].astype(F32), l0_ref[0]
    o1, l1 = sequence_order(o1_ref, so1_ref), sequence_order(l1_ref, sl1_ref)
    o2, l2 = sequence_order(o2_ref, so2_ref), sequence_order(l2_ref, sl2_ref)
    lm = jnp.maximum(jnp.maximum(l0, l1), l2)
    e0, e1, e2 = jnp.exp(l0 - lm), jnp.exp(l1 - lm), jnp.exp(l2 - lm)
    o = (e0 * o0 + e1 * o1 + e2 * o2) / (e0 + e1 + e2)
    ya = jnp.dot(o.astype(BF16), wa_ref[...], preferred_element_type=F32)
    merged = (jax.nn.sigmoid(ga_ref[0].astype(F32)) * yc + jax.nn.sigmoid(gb_ref[0].astype(F32)) * ya)
    out = jnp.dot(merged.astype(BF16), wo_ref[...], preferred_element_type=F32)
    x1 = x_ref[0] + mod_ref[0, 2:3, :] * out
    x1_ref[0] = x1
    h2 = _modulated_rmsnorm(x1, n2_ref[...], mod_ref[0, 3:4, :], mod_ref[0, 4:5, :])
    h2_ref[0] = h2.astype(h2_ref.dtype)


def _post(u2, os_, ls_, z3, off_g, x, mod3, norm2_g, wc, bc, wa, wo):
    b, s, d = x.shape
    ch = u2.shape[2]
    tm = 256
    gcol = off_g // d
    tok = lambda w: pl.BlockSpec((1, tm, w), lambda bi, i: (bi, i, 0))
    full = lambda a: pl.BlockSpec(a.shape, lambda bi, i: (0,) * a.ndim, pipeline_mode=pl.Buffered(1))

    def grp(a):
        if a.shape[1] == 1:
            return a.reshape(b, s, GROUP_WIDTH), tok(GROUP_WIDTH)
        dil = a.shape[1]
        return a, pl.BlockSpec((1, dil, tm // dil, GROUP_WIDTH), lambda bi, i: (bi, 0, i, 0))

    grp_args, grp_specs = zip(*[grp(a) for a in list(os_) + list(ls_)])
    n2 = norm2_g.reshape(1, d)
    bc2 = bc.reshape(1, d)
    return pl.pallas_call(
        _post_kernel,
        grid=(b, s // tm),
        in_specs=[tok(ch)] + list(grp_specs)
                 + [pl.BlockSpec((1, tm, d), lambda bi, i: (bi, i, gcol)),
                    pl.BlockSpec((1, tm, d), lambda bi, i: (bi, i, gcol + 1)),
                    tok(d),
                    pl.BlockSpec((1, 6, d), lambda bi, i: (bi, 0, 0)),
                    full(n2), full(wc), full(bc2), full(wa), full(wo)],
        out_specs=[tok(d), tok(d)],
        out_shape=[jax.ShapeDtypeStruct((b, s, d), F32), jax.ShapeDtypeStruct((b, s, d), BF16)],
        scratch_shapes=[pltpu.VMEM((GROUP_WIDTH // LANES, tm, LANES), F32)] * 4,
        compiler_params=_params("parallel", "parallel"),
        name="merge_out_proj",
    )(u2, *grp_args, z3, z3, x, mod3, n2, wc, bc2, wa, wo)


def _peer_scores_kernel(h_ref, wq_ref, keys_ref, s_ref):
    qp = jnp.dot(h_ref[...], wq_ref[...], preferred_element_type=F32).astype(BF16)
    nt = (((1,), (1,)), ((), ()))
    half = keys_ref.shape[2]
    for hp in range(keys_ref.shape[0]):
        s_ref[hp] = lax.dot_general(keys_ref[hp], qp[:, hp * half:(hp + 1) * half], nt,
                                    preferred_element_type=F32)


def _peer_scores(h2, wq, keys):
    t, d = h2.shape
    nhp, nk, half = keys.shape
    tm = 256
    return pl.pallas_call(
        _peer_scores_kernel,
        grid=(t // tm,),
        in_specs=[pl.BlockSpec((tm, d), lambda i: (i, 0)),
                  pl.BlockSpec(wq.shape, lambda i: (0, 0), pipeline_mode=pl.Buffered(1)),
                  pl.BlockSpec(keys.shape, lambda i: (0, 0, 0), pipeline_mode=pl.Buffered(1))],
        out_specs=pl.BlockSpec((nhp, nk, tm), lambda i: (0, 0, i)),
        out_shape=jax.ShapeDtypeStruct((nhp, nk, t), F32),
        compiler_params=_params("parallel"),
        name="peer_scores",
    )(h2, wq, keys)


def _top16(s, tie_safe):
    key = lax.broadcasted_iota(jnp.int32, s.shape, 0).astype(F32)
    slot = lax.broadcasted_iota(jnp.int32, (PEER_TOPK, s.shape[1]), 0)
    rank = jnp.full(s.shape, float(PEER_TOPK), F32)
    vals = jnp.zeros((PEER_TOPK, s.shape[1]), F32)
    for k in range(PEER_TOPK):
        m = jnp.max(s, axis=0, keepdims=True)
        sel = s == m
        if tie_safe:
            sel = key == jnp.min(jnp.where(sel, key, float(PEER_NKEYS)), axis=0, keepdims=True)
        rank = jnp.where(sel, float(k), rank)
        s = jnp.where(sel, -jnp.inf, s)
        vals = jnp.where(slot == k, m, vals)
    return vals, rank


def _ranks_are_a_top16(rank):
    expect = float(PEER_NKEYS * PEER_TOPK - PEER_TOPK * (PEER_TOPK + 1) // 2)
    return jnp.sum(rank, axis=0, keepdims=True) == expect


def _staircase_lengths(v1, v2, tie_safe):
    tl = v1.shape[1]
    j8 = lax.broadcasted_iota(jnp.int32, (8, tl), 0)
    j16 = lax.broadcasted_iota(jnp.int32, (PEER_TOPK, tl), 0)
    cands = [v1[0:1, :] + v2]
    flats = [j16]
    for i in range(1, 8):
        cands.append(jnp.where(j8 < PEER_TOPK // (i + 1), v1[i:i + 1, :] + v2[0:8, :], -jnp.inf))
        flats.append(j8 + i * PEER_TOPK)
    cands.append(v1[8:16, :] + v2[0:1, :])
    flats.append((j8 + 8) * PEER_TOPK)
    cand = jnp.concatenate(cands, axis=0)
    flat = jnp.concatenate(flats, axis=0).astype(F32)
    big = float(PEER_TOPK * PEER_TOPK)
    in_staircase = cand > -jnp.inf
    z = jnp.zeros((1, tl), F32)
    top0 = None
    for k in range(PEER_TOPK):
        m = jnp.max(cand, axis=0, keepdims=True)
        sel = cand == m
        if tie_safe:
            sel = flat == jnp.min(jnp.where(sel, flat, big), axis=0, keepdims=True)
        cand = jnp.where(sel, -jnp.inf, cand)
        if k == 0:
            top0 = m
        z = z + jnp.exp(m - top0)
    picked = jnp.where(jnp.logical_and(in_staircase, cand == -jnp.inf), 1.0, 0.0)
    ok = jnp.sum(picked, axis=0, keepdims=True) == float(PEER_TOPK)
    lens_lo = jnp.zeros((8, tl), F32)
    lens_lo = jnp.where(j8 == 0, jnp.sum(picked[0:16], axis=0, keepdims=True), lens_lo)
    for i in range(1, 8):
        lens_lo = jnp.where(j8 == i, jnp.sum(picked[8 + 8 * i:16 + 8 * i], axis=0, keepdims=True), lens_lo)
    return jnp.concatenate([lens_lo, picked[72:80]], axis=0), z, ok


SELECT_HEADS_PER_BLOCK = 4


def _select_heads(s_ref, e1_ref, len_ref, r2_ref, e2_ref, h0, tie_safe):
    ok = None
    for dh in range(SELECT_HEADS_PER_BLOCK):
        h = h0 + dh
        s1 = s_ref[2 * h]
        s2 = s_ref[2 * h + 1]
        v1, rank1 = _top16(s1, tie_safe)
        v2, rank2 = _top16(s2, tie_safe)
        lens, z, good = _staircase_lengths(v1, v2, tie_safe)
        for rank in (rank1, rank2):
            good = jnp.logical_and(good, _ranks_are_a_top16(rank))
        ok = good if ok is None else jnp.logical_and(ok, good)
        sel_len = jnp.zeros(rank1.shape, F32)
        for i in range(PEER_TOPK):
            sel_len = jnp.where(rank1 == float(i), lens[i:i + 1, :], sel_len)
        e1_ref[h] = jnp.exp(s1 - v1[0:1, :])
        len_ref[h] = sel_len
        r2_ref[h] = rank2.astype(r2_ref.dtype)
        e2_ref[h] = (jnp.exp(s2 - v2[0:1, :]) / z).astype(e2_ref.dtype)
    return ok


def _peer_select_kernel(s_ref, e1_ref, len_ref, r2_ref, e2_ref):
    def block(i, carry):
        h0 = i * SELECT_HEADS_PER_BLOCK
        ok = _select_heads(s_ref, e1_ref, len_ref, r2_ref, e2_ref, h0, tie_safe=False)

        @pl.when(jnp.sum(jnp.where(ok, 0.0, 1.0)) > 0.0)
        def _():
            _select_heads(s_ref, e1_ref, len_ref, r2_ref, e2_ref, h0, tie_safe=True)

        return carry

    lax.fori_loop(0, PEER_HEADS // SELECT_HEADS_PER_BLOCK, block, 0)


def _peer_select(scores_t):
    nhp, nk, t = scores_t.shape
    tl = 128
    out_spec = pl.BlockSpec((PEER_HEADS, nk, tl), lambda i: (0, 0, i))
    shp = lambda dt: jax.ShapeDtypeStruct((PEER_HEADS, nk, t), dt)
    return pl.pallas_call(
        _peer_select_kernel,
        grid=(t // tl,),
        in_specs=[pl.BlockSpec((nhp, nk, tl), lambda i: (0, 0, i))],
        out_specs=[out_spec] * 4,
        out_shape=[shp(F32), shp(F32), shp(BF16), shp(BF16)],
        compiler_params=_params("parallel"),
        name="peer_select",
    )(scores_t)


def _peer_dense_kernel(h_ref, wu_ref, wdt_ref, e1_ref, len_ref, r2_ref, e2_ref, x1_ref, mod_ref,
                       o_ref, acc_ref):
    j = pl.program_id(1)

    @pl.when(j == 0)
    def _():
        acc_ref[...] = jnp.zeros_like(acc_ref)

    nt = (((1,), (1,)), ((), ()))
    pre = lax.dot_general(wu_ref[...], h_ref[...], nt, preferred_element_type=F32)
    act = (0.5 * pre * (1.0 + lax.erf(pre * (0.5 ** 0.5)))).astype(BF16)
    nk = r2_ref.shape[1]
    parts = []
    for c in range(wu_ref.shape[0] // nk):
        gate = jnp.zeros((nk, h_ref.shape[0]), BF16)
        for h in range(PEER_HEADS):
            sel_len = len_ref[h, c:c + 1, :].astype(BF16)
            e1 = e1_ref[h, c:c + 1, :].astype(BF16)
            gate = gate + jnp.where(r2_ref[h] < sel_len, e1 * e2_ref[h], jnp.zeros((), BF16))
        parts.append(act[c * nk:(c + 1) * nk, :] * gate)
    p = jnp.concatenate(parts, axis=0)
    acc_ref[...] += jnp.dot(wdt_ref[...], p, preferred_element_type=F32)

    @pl.when(j == pl.num_programs(1) - 1)
    def _():
        o_ref[...] = x1_ref[...] + mod_ref[0, 5:6, :] * acc_ref[...].T


def _peer_dense(h2, wu, wdt, e1t, lent, r2t, e2t, x1, mod3):
    t, d = h2.shape
    ne = wu.shape[0]
    nk = r2t.shape[1]
    tl = 512
    te = 8 * nk
    small = pl.BlockSpec((PEER_HEADS, 8, tl), lambda i, j: (0, j, i))
    table = pl.BlockSpec((PEER_HEADS, nk, tl), lambda i, j: (0, 0, i))
    per_b = t // mod3.shape[0] // tl
    once = pl.Buffered(1)
    return pl.pallas_call(
        _peer_dense_kernel,
        grid=(t // tl, ne // te),
        in_specs=[pl.BlockSpec((tl, d), lambda i, j: (i, 0), pipeline_mode=once),
                  pl.BlockSpec((te, d), lambda i, j: (j, 0)),
                  pl.BlockSpec((d, te), lambda i, j: (0, j)),
                  small, small, table, table,
                  pl.BlockSpec((tl, d), lambda i, j: (i, 0), pipeline_mode=once),
                  pl.BlockSpec((1, 6, d), lambda i, j: (i // per_b, 0, 0))],
        out_specs=pl.BlockSpec((tl, d), lambda i, j: (i, 0)),
        out_shape=jax.ShapeDtypeStruct((t, d), F32),
        scratch_shapes=[pltpu.VMEM((d, tl), F32)],
        compiler_params=_params("parallel", "arbitrary"),
        name="peer_dense",
    )(h2, wu, wdt, e1t, lent, r2t, e2t, x1, mod3)


def kernel(x, c, norm1_g, norm2_g, w_ada, b_ada, w_in, conv_dw, conv_db, conv_ln_g, conv_ln_b,
           w_conv_out, b_conv_out, q_norm_g, k_norm_g, w_attn_o, w_out,
           peer_w_q, peer_sub_keys, peer_w_up, peer_w_down):
    b, s, d = x.shape
    t = b * s
    depth = w_ada.shape[0]
    conv_ch = conv_dw.shape[2]
    in_cols = w_in.shape[2]
    tn = GROUP_WIDTH
    ngrp = len(DILATIONS)
    glu_t = 2 * conv_ch // tn
    qkv_t = lambda gi: [glu_t + gi, glu_t + ngrp + gi, glu_t + 2 * ngrp + gi]
    gate_t = list(range(glu_t + 3 * ngrp, in_cols // tn))
    main_t = list(range(glu_t)) + gate_t + qkv_t(0)
    off_g = glu_t * tn
    for l in range(depth):
        mod3 = _ada(c, w_ada[l], b_ada[l]).reshape(b, 6, d)
        hs = _prenorm(x, norm1_g[l], mod3)
        w_in_b = w_in[l].astype(BF16)
        z = _in_proj(hs[0].reshape(t, d), w_in_b, 1024, tn, main_t, "in_proj")
        z3 = z.reshape(b, s, len(main_t) * tn)
        u2 = _conv_branch(z3, conv_dw[l], conv_db[l], conv_ln_g[l], conv_ln_b[l])
        q_g = q_norm_g[l].reshape(ngrp, 1, GROUP_WIDTH)
        k_g = k_norm_g[l].reshape(ngrp, 1, GROUP_WIDTH)
        outs, lses = [], []
        for gi, dil in enumerate(DILATIONS):
            if dil == 1:
                zg, cq = z3, len(main_t) - 3
            else:
                zg = _in_proj(hs[gi].reshape(t, d), w_in_b, 1024, tn, qkv_t(gi), f"in_proj_dil{dil}")
                zg, cq = zg.reshape(b * dil, s // dil, 3 * tn), 0
            o, lse = _attn_group(zg, q_g[gi], k_g[gi], dil, cq, cq + 1, cq + 2)
            outs.append(o.reshape(b, dil, s // dil, GROUP_WIDTH))
            lses.append(lse.reshape(b, dil, s // dil, GROUP_WIDTH))
        x1, h2 = _post(u2, outs, lses, z3, off_g, x, mod3, norm2_g[l],
                       w_conv_out[l].astype(BF16), b_conv_out[l], w_attn_o[l].astype(BF16),
                       w_out[l].astype(BF16))
        h2 = h2.reshape(t, d)
        keys = peer_sub_keys[l].reshape(2 * PEER_HEADS, PEER_NKEYS, -1).astype(BF16)
        scores_t = _peer_scores(h2, peer_w_q[l].astype(BF16), keys)
        e1t, lent, r2t, e2t = _peer_select(scores_t)
        x = _peer_dense(h2, peer_w_up[l].astype(BF16), peer_w_down[l].T.astype(BF16),
                        e1t, lent, r2t, e2t, x1.reshape(t, d), mod3).reshape(b, s, d)
    return x
```

```python
import functools

import jax
import jax.numpy as jnp
from jax import lax
from jax.experimental import pallas as pl
from jax.experimental.pallas import tpu as pltpu

F32 = jnp.float32
BF16 = jnp.bfloat16

EPS = 1e-6
CONV_K = 31
HEAD_DIM = 128
HEADS_PER_GROUP = 4
GROUP_WIDTH = HEADS_PER_GROUP * HEAD_DIM
DILATIONS = (1, 4, 16)
BAND = 128
PEER_HEADS = 8
PEER_NKEYS = 128
PEER_TOPK = 16
HALO = 32
LANES = 128
SUBLANES = 8

VMEM_LIMIT = 56 * 1024 * 1024


def _params(*sem):
    return pltpu.CompilerParams(dimension_semantics=sem, vmem_limit_bytes=VMEM_LIMIT)


def _ada_kernel(ct_ref, w_ref, b_ref, o_ref):
    d, tn = w_ref.shape
    nb = ct_ref.shape[1]
    rows = 8

    def body(i, accs):
        r0 = pl.multiple_of(i * rows, rows)
        w8 = w_ref[pl.ds(r0, rows), :]
        c8 = ct_ref[pl.ds(r0, rows), :]
        c8 = c8 * jax.nn.sigmoid(c8)
        return tuple(a + w8 * c8[:, b:b + 1] for b, a in enumerate(accs))

    accs = lax.fori_loop(0, d // rows, body,
                         tuple(jnp.zeros((rows, tn), F32) for _ in range(nb)), unroll=8)
    out = jnp.concatenate([jnp.sum(a, axis=0, keepdims=True) for a in accs], axis=0)
    o_ref[...] = out + b_ref[...]


def _ada(c, w_ada, b_ada):
    nb, d = c.shape
    n = w_ada.shape[1]
    tn = 1024
    return pl.pallas_call(
        _ada_kernel,
        grid=(n // tn,),
        in_specs=[pl.BlockSpec((d, nb), lambda j: (0, 0)),
                  pl.BlockSpec((d, tn), lambda j: (0, j)),
                  pl.BlockSpec((1, tn), lambda j: (0, j))],
        out_specs=pl.BlockSpec((nb, tn), lambda j: (0, j)),
        out_shape=jax.ShapeDtypeStruct((nb, n), F32),
        compiler_params=_params("arbitrary"),
        name="ada_mod",
    )(c.T, w_ada, b_ada.reshape(1, n))


def _modulated_rmsnorm(x, g, shift, scale):
    y = x * lax.rsqrt(jnp.mean(x * x, axis=-1, keepdims=True) + EPS) * g
    return y * (1.0 + scale) + shift


def _prenorm_kernel(x_ref, g_ref, mod_ref, o_ref, *rest):
    perm_refs, h_ref = rest[:-1], rest[-1]
    ts = x_ref.shape[1]
    h = _modulated_rmsnorm(x_ref[0], g_ref[...], mod_ref[0, 0:1, :], mod_ref[0, 1:2, :])
    o_ref[0] = h.astype(o_ref.dtype)
    for c in range(h_ref.shape[0]):
        h_ref[c] = h[:, c * LANES:(c + 1) * LANES]
    for p_ref in perm_refs:
        dil = p_ref.shape[1]
        for r in range(dil):
            for c in range(h_ref.shape[0]):
                rows = h_ref[c, pl.ds(r, ts // dil, stride=dil), :]
                p_ref[0, r, :, c * LANES:(c + 1) * LANES] = rows.astype(p_ref.dtype)


def _prenorm(x, g, mod3):
    b, s, d = x.shape
    ts = 512
    dils = [dil for dil in DILATIONS if dil > 1]
    nat = pl.BlockSpec((1, ts, d), lambda bi, i: (bi, i, 0))
    return pl.pallas_call(
        _prenorm_kernel,
        grid=(b, s // ts),
        in_specs=[nat,
                  pl.BlockSpec((1, d), lambda bi, i: (0, 0)),
                  pl.BlockSpec((1, 6, d), lambda bi, i: (bi, 0, 0))],
        out_specs=[nat] + [pl.BlockSpec((1, dil, ts // dil, d), lambda bi, i: (bi, 0, i, 0)) for dil in dils],
        out_shape=[jax.ShapeDtypeStruct((b, s, d), BF16)]
                  + [jax.ShapeDtypeStruct((b, dil, s // dil, d), BF16) for dil in dils],
        scratch_shapes=[pltpu.VMEM((d // LANES, ts, LANES), F32)],
        compiler_params=_params("parallel", "parallel"),
        name="prenorm",
    )(x, g.reshape(1, d), mod3)


IN_PROJ_ROWS = 2048


def _matmul_kernel(cols_ref, a_ref, b_ref, o_ref):
    del cols_ref
    o_ref[...] = jnp.dot(a_ref[...], b_ref[...].astype(a_ref.dtype),
                         preferred_element_type=F32).astype(o_ref.dtype)


def _in_proj(a, b, tm, tn, col_tiles, name):
    m, k = a.shape
    nt = len(col_tiles)
    return pl.pallas_call(
        _matmul_kernel,
        grid_spec=pltpu.PrefetchScalarGridSpec(
            num_scalar_prefetch=1,
            grid=(m // tm, nt),
            in_specs=[pl.BlockSpec((tm, k), lambda i, j, cols: (i, 0)),
                      pl.BlockSpec((k, tn), lambda i, j, cols: (0, cols[j]))],
            out_specs=pl.BlockSpec((tm, tn), lambda i, j, cols: (i, j))),
        out_shape=jax.ShapeDtypeStruct((m, nt * tn), BF16),
        compiler_params=_params("parallel", "arbitrary"),
        name=name,
    )(jnp.asarray(col_tiles, jnp.int32), a, b)


def _conv_kernel(ac_ref, gc_ref, ap_ref, gp_ref, dw_ref, db_ref, lg_ref, lb_ref, o_ref, buf_ref, shift_ref):
    ts = ac_ref.shape[1]
    i = pl.program_id(1)

    def glu(a, g):
        return a.astype(F32) * jax.nn.sigmoid(g.astype(F32))

    halo = glu(ap_ref[0], gp_ref[0])
    buf_ref[0:HALO, :] = jnp.where(i > 0, halo, 0.0)
    buf_ref[HALO:, :] = glu(ac_ref[0], gc_ref[0])
    span = shift_ref.shape[1]
    for b in range(1, SUBLANES):
        shift_ref[b - 1] = buf_ref[b:b + span, :]
    rows = 32
    first = HALO - (CONV_K - 1)
    for r in range(ts // rows):
        acc = jnp.zeros((rows, ac_ref.shape[2]), F32)
        for j in range(CONV_K):
            b = (first + j) % SUBLANES
            lo = r * rows + first + j - b
            window = buf_ref[lo:lo + rows, :] if b == 0 else shift_ref[b - 1, lo:lo + rows, :]
            acc = acc + dw_ref[j:j + 1, :] * window
        acc = acc + db_ref[...]
        mu = jnp.mean(acc, axis=-1, keepdims=True)
        cen = acc - mu
        var = jnp.mean(cen * cen, axis=-1, keepdims=True)
        y = cen * lax.rsqrt(var + EPS) * lg_ref[...] + lb_ref[...]
        o_ref[0, r * rows:(r + 1) * rows, :] = (y * jax.nn.sigmoid(y)).astype(o_ref.dtype)


def _conv_branch(z3, conv_dw, conv_db, ln_g, ln_b):
    b, s, _ = z3.shape
    ch = conv_dw.shape[1]
    ts = 256
    per = ts // HALO
    prev = lambda col: (lambda bi, i: (bi, jnp.maximum(i * per - 1, 0), col))
    vec = pl.BlockSpec((1, ch), lambda bi, i: (0, 0))
    return pl.pallas_call(
        _conv_kernel,
        grid=(b, s // ts),
        in_specs=[pl.BlockSpec((1, ts, ch), lambda bi, i: (bi, i, 0)),
                  pl.BlockSpec((1, ts, ch), lambda bi, i: (bi, i, 1)),
                  pl.BlockSpec((1, HALO, ch), prev(0)),
                  pl.BlockSpec((1, HALO, ch), prev(1)),
                  pl.BlockSpec((CONV_K, ch), lambda bi, i: (0, 0)),
                  vec, vec, vec],
        out_specs=pl.BlockSpec((1, ts, ch), lambda bi, i: (bi, i, 0)),
        out_shape=jax.ShapeDtypeStruct((b, s, ch), BF16),
        scratch_shapes=[pltpu.VMEM((ts + HALO, ch), F32),
                        pltpu.VMEM((SUBLANES - 1, ts + HALO - SUBLANES, ch), F32)],
        compiler_params=_params("parallel", "parallel"),
        name="conv_branch",
    )(z3, z3, z3, z3, conv_dw, conv_db.reshape(1, ch), ln_g.reshape(1, ch), ln_b.reshape(1, ch))


def _attn_kernel(q_ref, k_ref, v_ref, kh_ref, vh_ref, qg_ref, kg_ref, o_ref, l_ref):
    n = pl.program_id(1)
    qi = lax.broadcasted_iota(jnp.int32, (BAND, BAND), 0)
    ki = lax.broadcasted_iota(jnp.int32, (BAND, BAND), 1)
    mask_cur = ki <= qi
    mask_prev = ki >= qi
    mask_halo = ki >= qi + jnp.where(n > 0, 0, 2 * BAND)

    def headnorm(t, g):
        t = t.astype(F32)
        return t * lax.rsqrt(jnp.mean(t * t, axis=-1, keepdims=True) + EPS) * g

    nt = (((1,), (1,)), ((), ()))
    for j in range(HEADS_PER_GROUP):
        sl = slice(j * HEAD_DIM, (j + 1) * HEAD_DIM)
        qn = (headnorm(q_ref[0, :, sl], qg_ref[:, sl]) * (HEAD_DIM ** -0.5)).astype(BF16)
        kn = headnorm(k_ref[0, :, sl], kg_ref[:, sl]).astype(BF16)
        kh = headnorm(kh_ref[0, :, sl], kg_ref[:, sl]).astype(BF16)
        for blk in range(q_ref.shape[1] // BAND):
            rows = slice(blk * BAND, (blk + 1) * BAND)
            back = slice((blk - 1) * BAND, blk * BAND)
            q, kc, vc = qn[rows], kn[rows], v_ref[0, rows, sl]
            if blk == 0:
                kp, vp, mp = kh, vh_ref[0, :, sl], mask_halo
            else:
                kp, vp, mp = kn[back], v_ref[0, back, sl], mask_prev
            sp = jnp.where(mp, lax.dot_general(q, kp, nt, preferred_element_type=F32), -jnp.inf)
            sc = jnp.where(mask_cur, lax.dot_general(q, kc, nt, preferred_element_type=F32), -jnp.inf)
            m = jnp.maximum(jnp.max(sp, axis=-1, keepdims=True), jnp.max(sc, axis=-1, keepdims=True))
            pp = jnp.exp(sp - m)
            pc = jnp.exp(sc - m)
            den = jnp.sum(pp, axis=-1, keepdims=True) + jnp.sum(pc, axis=-1, keepdims=True)
            o = (jnp.dot(pp.astype(BF16), vp, preferred_element_type=F32)
                 + jnp.dot(pc.astype(BF16), vc, preferred_element_type=F32))
            o_ref[0, rows, sl] = (o / den).astype(o_ref.dtype)
            l_ref[0, rows, sl] = jnp.broadcast_to(m + jnp.log(den), (BAND, HEAD_DIM))


ATTN_TILE_ROWS = 512


def _attn_group(zg, q_g, k_g, dil, cq, ck, cv):
    nseq, l, _ = zg.shape
    rows = min(l, ATTN_TILE_ROWS)
    per = rows // BAND
    tile = lambda c: pl.BlockSpec((1, rows, GROUP_WIDTH), lambda si, n: (si, n, c))
    halo = lambda c: pl.BlockSpec((1, BAND, GROUP_WIDTH), lambda si, n: (si, jnp.maximum(n * per - 1, 0), c))
    gain = pl.BlockSpec((1, GROUP_WIDTH), lambda si, n: (0, 0))
    out_spec = tile(0)
    return pl.pallas_call(
        _attn_kernel,
        grid=(nseq, l // rows),
        in_specs=[tile(cq), tile(ck), tile(cv), halo(ck), halo(cv), gain, gain],
        out_specs=[out_spec, out_spec],
        out_shape=[jax.ShapeDtypeStruct((nseq, l, GROUP_WIDTH), BF16),
                   jax.ShapeDtypeStruct((nseq, l, GROUP_WIDTH), F32)],
        compiler_params=_params("parallel", "arbitrary"),
        name=f"dilated_attn_{dil}",
    )(zg, zg, zg, zg, zg, q_g, k_g)


def _post_kernel(u_ref, o0_ref, o1_ref, o2_ref, l0_ref, l1_ref, l2_ref, ga_ref, gb_ref, x_ref,
                 mod_ref, n2_ref, wc_ref, bc_ref, wa_ref, wo_ref, x1_ref, h2_ref,
                 so1_ref, sl1_ref, so2_ref, sl2_ref):
    yc = jnp.dot(u_ref[0], wc_ref[...], preferred_element_type=F32) + bc_ref[...]

    def sequence_order(src_ref, dst_ref):
        dil, rows = src_ref.shape[1], src_ref.shape[2]
        nch = dst_ref.shape[0]
        for r in range(dil):
            src = src_ref[0, r].astype(F32)
            for c in range(nch):
                dst_ref[c, pl.ds(r, rows, stride=dil), :] = src[:, c * LANES:(c + 1) * LANES]
        return jnp.concatenate([dst_ref[c] for c in range(nch)], axis=1)

    o0, l0 = o0_ref[0].astype(F32), l0_ref[0]
    o1, l1 = sequence_order(o1_ref, so1_ref), sequence_order(l1_ref, sl1_ref)
    o2, l2 = sequence_order(o2_ref, so2_ref), sequence_order(l2_ref, sl2_ref)
    lm = jnp.maximum(jnp.maximum(l0, l1), l2)
    e0, e1, e2 = jnp.exp(l0 - lm), jnp.exp(l1 - lm), jnp.exp(l2 - lm)
    o = (e0 * o0 + e1 * o1 + e2 * o2) / (e0 + e1 + e2)
    ya = jnp.dot(o.astype(BF16), wa_ref[...], preferred_element_type=F32)
    merged = (jax.nn.sigmoid(ga_ref[0].astype(F32)) * yc + jax.nn.sigmoid(gb_ref[0].astype(F32)) * ya)
    out = jnp.dot(merged.astype(BF16), wo_ref[...], preferred_element_type=F32)
    x1 = x_ref[0] + mod_ref[0, 2:3, :] * out
    x1_ref[0] = x1
    h2 = _modulated_rmsnorm(x1, n2_ref[...], mod_ref[0, 3:4, :], mod_ref[0, 4:5, :])
    h2_ref[0] = h2.astype(h2_ref.dtype)


def _post(u2, os_, ls_, z3, off_g, x, mod3, norm2_g, wc, bc, wa, wo):
    b, s, d = x.shape
    ch = u2.shape[2]
    tm = 256
    gcol = off_g // d
    tok = lambda w: pl.BlockSpec((1, tm, w), lambda bi, i: (bi, i, 0))
    full = lambda a: pl.BlockSpec(a.shape, lambda bi, i: (0,) * a.ndim, pipeline_mode=pl.Buffered(1))

    def grp(a):
        if a.shape[1] == 1:
            return a.reshape(b, s, GROUP_WIDTH), tok(GROUP_WIDTH)
        dil = a.shape[1]
        return a, pl.BlockSpec((1, dil, tm // dil, GROUP_WIDTH), lambda bi, i: (bi, 0, i, 0))

    grp_args, grp_specs = zip(*[grp(a) for a in list(os_) + list(ls_)])
    n2 = norm2_g.reshape(1, d)
    bc2 = bc.reshape(1, d)
    return pl.pallas_call(
        _post_kernel,
        grid=(b, s // tm),
        in_specs=[tok(ch)] + list(grp_specs)
                 + [pl.BlockSpec((1, tm, d), lambda bi, i: (bi, i, gcol)),
                    pl.BlockSpec((1, tm, d), lambda bi, i: (bi, i, gcol + 1)),
                    tok(d),
                    pl.BlockSpec((1, 6, d), lambda bi, i: (bi, 0, 0)),
                    full(n2), full(wc), full(bc2), full(wa), full(wo)],
        out_specs=[tok(d), tok(d)],
        out_shape=[jax.ShapeDtypeStruct((b, s, d), F32), jax.ShapeDtypeStruct((b, s, d), BF16)],
        scratch_shapes=[pltpu.VMEM((GROUP_WIDTH // LANES, tm, LANES), F32)] * 4,
        compiler_params=_params("parallel", "parallel"),
        name="merge_out_proj",
    )(u2, *grp_args, z3, z3, x, mod3, n2, wc, bc2, wa, wo)


def _peer_scores_kernel(h_ref, wq_ref, keys_ref, s_ref):
    qp = jnp.dot(h_ref[...], wq_ref[...], preferred_element_type=F32).astype(BF16)
    nt = (((1,), (1,)), ((), ()))
    half = keys_ref.shape[2]
    for hp in range(keys_ref.shape[0]):
        s_ref[hp] = lax.dot_general(keys_ref[hp], qp[:, hp * half:(hp + 1) * half], nt,
                                    preferred_element_type=F32)


def _peer_scores(h2, wq, keys):
    t, d = h2.shape
    nhp, nk, half = keys.shape
    tm = 256
    return pl.pallas_call(
        _peer_scores_kernel,
        grid=(t // tm,),
        in_specs=[pl.BlockSpec((tm, d), lambda i: (i, 0)),
                  pl.BlockSpec(wq.shape, lambda i: (0, 0), pipeline_mode=pl.Buffered(1)),
                  pl.BlockSpec(keys.shape, lambda i: (0, 0, 0), pipeline_mode=pl.Buffered(1))],
        out_specs=pl.BlockSpec((nhp, nk, tm), lambda i: (0, 0, i)),
        out_shape=jax.ShapeDtypeStruct((nhp, nk, t), F32),
        compiler_params=_params("parallel"),
        name="peer_scores",
    )(h2, wq, keys)


def _top16(s, tie_safe):
    key = lax.broadcasted_iota(jnp.int32, s.shape, 0).astype(F32)
    slot = lax.broadcasted_iota(jnp.int32, (PEER_TOPK, s.shape[1]), 0)
    rank = jnp.full(s.shape, float(PEER_TOPK), F32)
    vals = jnp.zeros((PEER_TOPK, s.shape[1]), F32)
    for k in range(PEER_TOPK):
        m = jnp.max(s, axis=0, keepdims=True)
        sel = s == m
        if tie_safe:
            sel = key == jnp.min(jnp.where(sel, key, float(PEER_NKEYS)), axis=0, keepdims=True)
        rank = jnp.where(sel, float(k), rank)
        s = jnp.where(sel, -jnp.inf, s)
        vals = jnp.where(slot == k, m, vals)
    return vals, rank


def _ranks_are_a_top16(rank):
    expect = float(PEER_NKEYS * PEER_TOPK - PEER_TOPK * (PEER_TOPK + 1) // 2)
    return jnp.sum(rank, axis=0, keepdims=True) == expect


def _staircase_lengths(v1, v2, tie_safe):
    tl = v1.shape[1]
    j8 = lax.broadcasted_iota(jnp.int32, (8, tl), 0)
    j16 = lax.broadcasted_iota(jnp.int32, (PEER_TOPK, tl), 0)
    cands = [v1[0:1, :] + v2]
    flats = [j16]
    for i in range(1, 8):
        cands.append(jnp.where(j8 < PEER_TOPK // (i + 1), v1[i:i + 1, :] + v2[0:8, :], -jnp.inf))
        flats.append(j8 + i * PEER_TOPK)
    cands.append(v1[8:16, :] + v2[0:1, :])
    flats.append((j8 + 8) * PEER_TOPK)
    cand = jnp.concatenate(cands, axis=0)
    flat = jnp.concatenate(flats, axis=0).astype(F32)
    big = float(PEER_TOPK * PEER_TOPK)
    in_staircase = cand > -jnp.inf
    z = jnp.zeros((1, tl), F32)
    top0 = None
    for k in range(PEER_TOPK):
        m = jnp.max(cand, axis=0, keepdims=True)
        sel = cand == m
        if tie_safe:
            sel = flat == jnp.min(jnp.where(sel, flat, big), axis=0, keepdims=True)
        cand = jnp.where(sel, -jnp.inf, cand)
        if k == 0:
            top0 = m
        z = z + jnp.exp(m - top0)
    picked = jnp.where(jnp.logical_and(in_staircase, cand == -jnp.inf), 1.0, 0.0)
    ok = jnp.sum(picked, axis=0, keepdims=True) == float(PEER_TOPK)
    lens_lo = jnp.zeros((8, tl), F32)
    lens_lo = jnp.where(j8 == 0, jnp.sum(picked[0:16], axis=0, keepdims=True), lens_lo)
    for i in range(1, 8):
        lens_lo = jnp.where(j8 == i, jnp.sum(picked[8 + 8 * i:16 + 8 * i], axis=0, keepdims=True), lens_lo)
    return jnp.concatenate([lens_lo, picked[72:80]], axis=0), z, ok


SELECT_HEADS_PER_BLOCK = 4


def _select_heads(s_ref, e1_ref, len_ref, r2_ref, e2_ref, h0, tie_safe):
    ok = None
    for dh in range(SELECT_HEADS_PER_BLOCK):
        h = h0 + dh
        s1 = s_ref[2 * h]
        s2 = s_ref[2 * h + 1]
        v1, rank1 = _top16(s1, tie_safe)
        v2, rank2 = _top16(s2, tie_safe)
        lens, z, good = _staircase_lengths(v1, v2, tie_safe)
        for rank in (rank1, rank2):
            good = jnp.logical_and(good, _ranks_are_a_top16(rank))
        ok = good if ok is None else jnp.logical_and(ok, good)
        sel_len = jnp.zeros(rank1.shape, F32)
        for i in range(PEER_TOPK):
            sel_len = jnp.where(rank1 == float(i), lens[i:i + 1, :], sel_len)
        e1_ref[h] = jnp.exp(s1 - v1[0:1, :])
        len_ref[h] = sel_len
        r2_ref[h] = rank2.astype(r2_ref.dtype)
        e2_ref[h] = (jnp.exp(s2 - v2[0:1, :]) / z).astype(e2_ref.dtype)
    return ok


def _peer_select_kernel(s_ref, e1_ref, len_ref, r2_ref, e2_ref):
    def block(i, carry):
        h0 = i * SELECT_HEADS_PER_BLOCK
        ok = _select_heads(s_ref, e1_ref, len_ref, r2_ref, e2_ref, h0, tie_safe=False)

        @pl.when(jnp.sum(jnp.where(ok, 0.0, 1.0)) > 0.0)
        def _():
            _select_heads(s_ref, e1_ref, len_ref, r2_ref, e2_ref, h0, tie_safe=True)

        return carry

    lax.fori_loop(0, PEER_HEADS // SELECT_HEADS_PER_BLOCK, block, 0)


def _peer_select(scores_t):
    nhp, nk, t = scores_t.shape
    tl = 128
    out_spec = pl.BlockSpec((PEER_HEADS, nk, tl), lambda i: (0, 0, i))
    shp = lambda dt: jax.ShapeDtypeStruct((PEER_HEADS, nk, t), dt)
    return pl.pallas_call(
        _peer_select_kernel,
        grid=(t // tl,),
        in_specs=[pl.BlockSpec((nhp, nk, tl), lambda i: (0, 0, i))],
        out_specs=[out_spec] * 4,
        out_shape=[shp(F32), shp(F32), shp(BF16), shp(BF16)],
        compiler_params=_params("parallel"),
        name="peer_select",
    )(scores_t)


def _peer_dense_kernel(h_ref, wu_ref, wdt_ref, e1_ref, len_ref, r2_ref, e2_ref, x1_ref, mod_ref,
                       o_ref, acc_ref):
    j = pl.program_id(1)

    @pl.when(j == 0)
    def _():
        acc_ref[...] = jnp.zeros_like(acc_ref)

    nt = (((1,), (1,)), ((), ()))
    pre = lax.dot_general(wu_ref[...], h_ref[...], nt, preferred_element_type=F32)
    act = (0.5 * pre * (1.0 + lax.erf(pre * (0.5 ** 0.5)))).astype(BF16)
    nk = r2_ref.shape[1]
    parts = []
    for c in range(wu_ref.shape[0] // nk):
        gate = jnp.zeros((nk, h_ref.shape[0]), BF16)
        for h in range(PEER_HEADS):
            sel_len = len_ref[h, c:c + 1, :].astype(BF16)
            e1 = e1_ref[h, c:c + 1, :].astype(BF16)
            gate = gate + jnp.where(r2_ref[h] < sel_len, e2_ref[h], jnp.zeros((), BF16)) * e1
        parts.append(act[c * nk:(c + 1) * nk, :] * gate)
    p = jnp.concatenate(parts, axis=0)
    acc_ref[...] += jnp.dot(wdt_ref[...], p, preferred_element_type=F32)

    @pl.when(j == pl.num_programs(1) - 1)
    def _():
        o_ref[...] = x1_ref[...] + mod_ref[0, 5:6, :] * acc_ref[...].T


def _peer_dense(h2, wu, wdt, e1t, lent, r2t, e2t, x1, mod3):
    t, d = h2.shape
    ne = wu.shape[0]
    nk = r2t.shape[1]
    tl = 512
    te = 8 * nk
    small = pl.BlockSpec((PEER_HEADS, 8, tl), lambda i, j: (0, j, i))
    table = pl.BlockSpec((PEER_HEADS, nk, tl), lambda i, j: (0, 0, i))
    per_b = t // mod3.shape[0] // tl
    once = pl.Buffered(1)
    return pl.pallas_call(
        _peer_dense_kernel,
        grid=(t // tl, ne // te),
        in_specs=[pl.BlockSpec((tl, d), lambda i, j: (i, 0), pipeline_mode=once),
                  pl.BlockSpec((te, d), lambda i, j: (j, 0)),
                  pl.BlockSpec((d, te), lambda i, j: (0, j)),
                  small, small, table, table,
                  pl.BlockSpec((tl, d), lambda i, j: (i, 0), pipeline_mode=once),
                  pl.BlockSpec((1, 6, d), lambda i, j: (i // per_b, 0, 0))],
        out_specs=pl.BlockSpec((tl, d), lambda i, j: (i, 0)),
        out_shape=jax.ShapeDtypeStruct((t, d), F32),
        scratch_shapes=[pltpu.VMEM((d, tl), F32)],
        compiler_params=_params("parallel", "arbitrary"),
        name="peer_dense",
    )(h2, wu, wdt, e1t, lent, r2t, e2t, x1, mod3)


def kernel(x, c, norm1_g, norm2_g, w_ada, b_ada, w_in, conv_dw, conv_db, conv_ln_g, conv_ln_b,
           w_conv_out, b_conv_out, q_norm_g, k_norm_g, w_attn_o, w_out,
           peer_w_q, peer_sub_keys, peer_w_up, peer_w_down):
    b, s, d = x.shape
    t = b * s
    depth = w_ada.shape[0]
    conv_ch = conv_dw.shape[2]
    in_cols = w_in.shape[2]
    tn = GROUP_WIDTH
    ngrp = len(DILATIONS)
    glu_t = 2 * conv_ch // tn
    qkv_t = lambda gi: [glu_t + gi, glu_t + ngrp + gi, glu_t + 2 * ngrp + gi]
    gate_t = list(range(glu_t + 3 * ngrp, in_cols // tn))
    main_t = list(range(glu_t)) + gate_t + qkv_t(0)
    off_g = glu_t * tn
    for l in range(depth):
        mod3 = _ada(c, w_ada[l], b_ada[l]).reshape(b, 6, d)
        hs = _prenorm(x, norm1_g[l], mod3)
        z = _in_proj(hs[0].reshape(t, d), w_in[l], IN_PROJ_ROWS, tn, main_t, "in_proj")
        z3 = z.reshape(b, s, len(main_t) * tn)
        u2 = _conv_branch(z3, conv_dw[l], conv_db[l], conv_ln_g[l], conv_ln_b[l])
        q_g = q_norm_g[l].reshape(ngrp, 1, GROUP_WIDTH)
        k_g = k_norm_g[l].reshape(ngrp, 1, GROUP_WIDTH)
        outs, lses = [], []
        for gi, dil in enumerate(DILATIONS):
            if dil == 1:
                zg, cq = z3, len(main_t) - 3
            else:
                zg = _in_proj(hs[gi].reshape(t, d), w_in[l], IN_PROJ_ROWS, tn, qkv_t(gi), f"in_proj_dil{dil}")
                zg, cq = zg.reshape(b * dil, s // dil, 3 * tn), 0
            o, lse = _attn_group(zg, q_g[gi], k_g[gi], dil, cq, cq + 1, cq + 2)
            outs.append(o.reshape(b, dil, s // dil, GROUP_WIDTH))
            lses.append(lse.reshape(b, dil, s // dil, GROUP_WIDTH))
        x1, h2 = _post(u2, outs, lses, z3, off_g, x, mod3, norm2_g[l],
                       w_conv_out[l].astype(BF16), b_conv_out[l], w_attn_o[l].astype(BF16),
                       w_out[l].astype(BF16))
        h2 = h2.reshape(t, d)
        keys = peer_sub_keys[l].reshape(2 * PEER_HEADS, PEER_NKEYS, -1).astype(BF16)
        scores_t = _peer_scores(h2, peer_w_q[l].astype(BF16), keys)
        e1t, lent, r2t, e2t = _peer_select(scores_t)
        x = _peer_dense(h2, peer_w_up[l].astype(BF16), peer_w_down[l].T.astype(BF16),
                        e1t, lent, r2t, e2t, x1.reshape(t, d), mod3).reshape(b, s, d)
    return x
```

```python
import functools

import jax
import jax.numpy as jnp
from jax import lax
from jax.experimental import pallas as pl
from jax.experimental.pallas import tpu as pltpu

F32 = jnp.float32
BF16 = jnp.bfloat16

EPS = 1e-6
CONV_K = 31
HEAD_DIM = 128
HEADS_PER_GROUP = 4
GROUP_WIDTH = HEADS_PER_GROUP * HEAD_DIM
DILATIONS = (1, 4, 16)
BAND = 128
PEER_HEADS = 8
PEER_NKEYS = 128
PEER_TOPK = 16
HALO = 32
LANES = 128
SUBLANES = 8

VMEM_LIMIT = 56 * 1024 * 1024


def _params(*sem):
    return pltpu.CompilerParams(dimension_semantics=sem, vmem_limit_bytes=VMEM_LIMIT)


def _ada_kernel(ct_ref, w_ref, b_ref, o_ref):
    d, tn = w_ref.shape
    nb = ct_ref.shape[1]
    rows = 8

    def body(i, accs):
        r0 = pl.multiple_of(i * rows, rows)
        w8 = w_ref[pl.ds(r0, rows), :]
        c8 = ct_ref[pl.ds(r0, rows), :]
        c8 = c8 * jax.nn.sigmoid(c8)
        return tuple(a + w8 * c8[:, b:b + 1] for b, a in enumerate(accs))

    accs = lax.fori_loop(0, d // rows, body,
                         tuple(jnp.zeros((rows, tn), F32) for _ in range(nb)), unroll=8)
    out = jnp.concatenate([jnp.sum(a, axis=0, keepdims=True) for a in accs], axis=0)
    o_ref[...] = out + b_ref[...]


def _ada(c, w_ada, b_ada):
    nb, d = c.shape
    n = w_ada.shape[1]
    tn = 1024
    return pl.pallas_call(
        _ada_kernel,
        grid=(n // tn,),
        in_specs=[pl.BlockSpec((d, nb), lambda j: (0, 0)),
                  pl.BlockSpec((d, tn), lambda j: (0, j)),
                  pl.BlockSpec((1, tn), lambda j: (0, j))],
        out_specs=pl.BlockSpec((nb, tn), lambda j: (0, j)),
        out_shape=jax.ShapeDtypeStruct((nb, n), F32),
        compiler_params=_params("arbitrary"),
        name="ada_mod",
    )(c.T, w_ada, b_ada.reshape(1, n))


def _modulated_rmsnorm(x, g, shift, scale):
    y = x * lax.rsqrt(jnp.mean(x * x, axis=-1, keepdims=True) + EPS) * g
    return y * (1.0 + scale) + shift


def _prenorm_kernel(x_ref, g_ref, mod_ref, o_ref, *rest):
    perm_refs, h_ref = rest[:-1], rest[-1]
    ts = x_ref.shape[1]
    h = _modulated_rmsnorm(x_ref[0], g_ref[...], mod_ref[0, 0:1, :], mod_ref[0, 1:2, :])
    o_ref[0] = h.astype(o_ref.dtype)
    for c in range(h_ref.shape[0]):
        h_ref[c] = h[:, c * LANES:(c + 1) * LANES]
    for p_ref in perm_refs:
        dil = p_ref.shape[1]
        for r in range(dil):
            for c in range(h_ref.shape[0]):
                rows = h_ref[c, pl.ds(r, ts // dil, stride=dil), :]
                p_ref[0, r, :, c * LANES:(c + 1) * LANES] = rows.astype(p_ref.dtype)


def _prenorm(x, g, mod3):
    b, s, d = x.shape
    ts = 512
    dils = [dil for dil in DILATIONS if dil > 1]
    nat = pl.BlockSpec((1, ts, d), lambda bi, i: (bi, i, 0))
    return pl.pallas_call(
        _prenorm_kernel,
        grid=(b, s // ts),
        in_specs=[nat,
                  pl.BlockSpec((1, d), lambda bi, i: (0, 0)),
                  pl.BlockSpec((1, 6, d), lambda bi, i: (bi, 0, 0))],
        out_specs=[nat] + [pl.BlockSpec((1, dil, ts // dil, d), lambda bi, i: (bi, 0, i, 0)) for dil in dils],
        out_shape=[jax.ShapeDtypeStruct((b, s, d), BF16)]
                  + [jax.ShapeDtypeStruct((b, dil, s // dil, d), BF16) for dil in dils],
        scratch_shapes=[pltpu.VMEM((d // LANES, ts, LANES), F32)],
        compiler_params=_params("parallel", "parallel"),
        name="prenorm",
    )(x, g.reshape(1, d), mod3)


IN_PROJ_ROWS = 2048


def _matmul_kernel(cols_ref, a_ref, b_ref, o_ref):
    del cols_ref
    o_ref[...] = jnp.dot(a_ref[...], b_ref[...].astype(a_ref.dtype),
                         preferred_element_type=F32).astype(o_ref.dtype)


def _in_proj(a, b, tm, tn, col_tiles, name):
    m, k = a.shape
    nt = len(col_tiles)
    return pl.pallas_call(
        _matmul_kernel,
        grid_spec=pltpu.PrefetchScalarGridSpec(
            num_scalar_prefetch=1,
            grid=(m // tm, nt),
            in_specs=[pl.BlockSpec((tm, k), lambda i, j, cols: (i, 0)),
                      pl.BlockSpec((k, tn), lambda i, j, cols: (0, cols[j]))],
            out_specs=pl.BlockSpec((tm, tn), lambda i, j, cols: (i, j))),
        out_shape=jax.ShapeDtypeStruct((m, nt * tn), BF16),
        compiler_params=_params("parallel", "arbitrary"),
        name=name,
    )(jnp.asarray(col_tiles, jnp.int32), a, b)


def _conv_kernel(ac_ref, gc_ref, ap_ref, gp_ref, dw_ref, db_ref, lg_ref, lb_ref, o_ref, buf_ref, shift_ref):
    ts = ac_ref.shape[1]
    i = pl.program_id(1)

    def glu(a, g):
        return a.astype(F32) * jax.nn.sigmoid(g.astype(F32))

    halo = glu(ap_ref[0], gp_ref[0])
    buf_ref[0:HALO, :] = jnp.where(i > 0, halo, 0.0)
    buf_ref[HALO:, :] = glu(ac_ref[0], gc_ref[0])
    span = shift_ref.shape[1]
    for b in range(1, SUBLANES):
        shift_ref[b - 1] = buf_ref[b:b + span, :]
    rows = 32
    first = HALO - (CONV_K - 1)
    for r in range(ts // rows):
        acc = jnp.zeros((rows, ac_ref.shape[2]), F32)
        for j in range(CONV_K):
            b = (first + j) % SUBLANES
            lo = r * rows + first + j - b
            window = buf_ref[lo:lo + rows, :] if b == 0 else shift_ref[b - 1, lo:lo + rows, :]
            acc = acc + dw_ref[j:j + 1, :] * window
        acc = acc + db_ref[...]
        mu = jnp.mean(acc, axis=-1, keepdims=True)
        cen = acc - mu
        var = jnp.mean(cen * cen, axis=-1, keepdims=True)
        y = cen * lax.rsqrt(var + EPS) * lg_ref[...] + lb_ref[...]
        o_ref[0, r * rows:(r + 1) * rows, :] = (y * jax.nn.sigmoid(y)).astype(o_ref.dtype)


def _conv_branch(z3, conv_dw, conv_db, ln_g, ln_b):
    b, s, _ = z3.shape
    ch = conv_dw.shape[1]
    ts = 512
    per = ts // HALO
    prev = lambda col: (lambda bi, i: (bi, jnp.maximum(i * per - 1, 0), col))
    vec = pl.BlockSpec((1, ch), lambda bi, i: (0, 0))
    return pl.pallas_call(
        _conv_kernel,
        grid=(b, s // ts),
        in_specs=[pl.BlockSpec((1, ts, ch), lambda bi, i: (bi, i, 0)),
                  pl.BlockSpec((1, ts, ch), lambda bi, i: (bi, i, 1)),
                  pl.BlockSpec((1, HALO, ch), prev(0)),
                  pl.BlockSpec((1, HALO, ch), prev(1)),
                  pl.BlockSpec((CONV_K, ch), lambda bi, i: (0, 0)),
                  vec, vec, vec],
        out_specs=pl.BlockSpec((1, ts, ch), lambda bi, i: (bi, i, 0)),
        out_shape=jax.ShapeDtypeStruct((b, s, ch), BF16),
        scratch_shapes=[pltpu.VMEM((ts + HALO, ch), F32),
                        pltpu.VMEM((SUBLANES - 1, ts + HALO - SUBLANES, ch), F32)],
        compiler_params=_params("parallel", "parallel"),
        name="conv_branch",
    )(z3, z3, z3, z3, conv_dw, conv_db.reshape(1, ch), ln_g.reshape(1, ch), ln_b.reshape(1, ch))


def _attn_kernel(q_ref, k_ref, v_ref, kh_ref, vh_ref, qg_ref, kg_ref, o_ref, l_ref):
    n = pl.program_id(1)
    qi = lax.broadcasted_iota(jnp.int32, (BAND, BAND), 0)
    ki = lax.broadcasted_iota(jnp.int32, (BAND, BAND), 1)
    mask_cur = ki <= qi
    mask_prev = ki >= qi
    mask_halo = ki >= qi + jnp.where(n > 0, 0, 2 * BAND)

    def headnorm(t, g):
        t = t.astype(F32)
        return t * lax.rsqrt(jnp.mean(t * t, axis=-1, keepdims=True) + EPS) * g

    nt = (((1,), (1,)), ((), ()))
    for j in range(HEADS_PER_GROUP):
        sl = slice(j * HEAD_DIM, (j + 1) * HEAD_DIM)
        qn = (headnorm(q_ref[0, :, sl], qg_ref[:, sl]) * (HEAD_DIM ** -0.5)).astype(BF16)
        kn = headnorm(k_ref[0, :, sl], kg_ref[:, sl]).astype(BF16)
        kh = headnorm(kh_ref[0, :, sl], kg_ref[:, sl]).astype(BF16)
        for blk in range(q_ref.shape[1] // BAND):
            rows = slice(blk * BAND, (blk + 1) * BAND)
            back = slice((blk - 1) * BAND, blk * BAND)
            q, kc, vc = qn[rows], kn[rows], v_ref[0, rows, sl]
            if blk == 0:
                kp, vp, mp = kh, vh_ref[0, :, sl], mask_halo
            else:
                kp, vp, mp = kn[back], v_ref[0, back, sl], mask_prev
            sp = jnp.where(mp, lax.dot_general(q, kp, nt, preferred_element_type=F32), -jnp.inf)
            sc = jnp.where(mask_cur, lax.dot_general(q, kc, nt, preferred_element_type=F32), -jnp.inf)
            m = jnp.maximum(jnp.max(sp, axis=-1, keepdims=True), jnp.max(sc, axis=-1, keepdims=True))
            pp = jnp.exp(sp - m)
            pc = jnp.exp(sc - m)
            den = jnp.sum(pp, axis=-1, keepdims=True) + jnp.sum(pc, axis=-1, keepdims=True)
            o = (jnp.dot(pp.astype(BF16), vp, preferred_element_type=F32)
                 + jnp.dot(pc.astype(BF16), vc, preferred_element_type=F32))
            o_ref[0, rows, sl] = (o / den).astype(o_ref.dtype)
            l_ref[0, rows, sl] = jnp.broadcast_to(m + jnp.log(den), (BAND, HEAD_DIM))


ATTN_TILE_ROWS = 512


def _attn_group(zg, q_g, k_g, dil, cq, ck, cv):
    nseq, l, _ = zg.shape
    rows = min(l, ATTN_TILE_ROWS)
    per = rows // BAND
    tile = lambda c: pl.BlockSpec((1, rows, GROUP_WIDTH), lambda si, n: (si, n, c))
    halo = lambda c: pl.BlockSpec((1, BAND, GROUP_WIDTH), lambda si, n: (si, jnp.maximum(n * per - 1, 0), c))
    gain = pl.BlockSpec((1, GROUP_WIDTH), lambda si, n: (0, 0))
    out_spec = tile(0)
    return pl.pallas_call(
        _attn_kernel,
        grid=(nseq, l // rows),
        in_specs=[tile(cq), tile(ck), tile(cv), halo(ck), halo(cv), gain, gain],
        out_specs=[out_spec, out_spec],
        out_shape=[jax.ShapeDtypeStruct((nseq, l, GROUP_WIDTH), BF16),
                   jax.ShapeDtypeStruct((nseq, l, GROUP_WIDTH), F32)],
        compiler_params=_params("parallel", "arbitrary"),
        name=f"dilated_attn_{dil}",
    )(zg, zg, zg, zg, zg, q_g, k_g)


def _post_kernel(u_ref, o0_ref, o1_ref, o2_ref, l0_ref, l1_ref, l2_ref, ga_ref, gb_ref, x_ref,
                 mod_ref, n2_ref, wc_ref, bc_ref, wa_ref, wo_ref, x1_ref, h2_ref,
                 so1_ref, sl1_ref, so2_ref, sl2_ref):
    yc = jnp.dot(u_ref[0], wc_ref[...], preferred_element_type=F32) + bc_ref[...]

    def sequence_order(src_ref, dst_ref):
        dil, rows = src_ref.shape[1], src_ref.shape[2]
        nch = dst_ref.shape[0]
        for r in range(dil):
            src = src_ref[0, r].astype(F32)
            for c in range(nch):
                dst_ref[c, pl.ds(r, rows, stride=dil), :] = src[:, c * LANES:(c + 1) * LANES]
        return jnp.concatenate([dst_ref[c] for c in range(nch)], axis=1)

    o0, l0 = o0_ref[0].astype(F32), l0_ref[0]
    o1, l1 = sequence_order(o1_ref, so1_ref), sequence_order(l1_ref, sl1_ref)
    o2, l2 = sequence_order(o2_ref, so2_ref), sequence_order(l2_ref, sl2_ref)
    lm = jnp.maximum(jnp.maximum(l0, l1), l2)
    e0, e1, e2 = jnp.exp(l0 - lm), jnp.exp(l1 - lm), jnp.exp(l2 - lm)
    o = (e0 * o0 + e1 * o1 + e2 * o2) / (e0 + e1 + e2)
    ya = jnp.dot(o.astype(BF16), wa_ref[...], preferred_element_type=F32)
    merged = (jax.nn.sigmoid(ga_ref[0].astype(F32)) * yc + jax.nn.sigmoid(gb_ref[0].astype(F32)) * ya)
    out = jnp.dot(merged.astype(BF16), wo_ref[...], preferred_element_type=F32)
    x1 = x_ref[0] + mod_ref[0, 2:3, :] * out
    x1_ref[0] = x1
    h2 = _modulated_rmsnorm(x1, n2_ref[...], mod_ref[0, 3:4, :], mod_ref[0, 4:5, :])
    h2_ref[0] = h2.astype(h2_ref.dtype)


def _post(u2, os_, ls_, z3, off_g, x, mod3, norm2_g, wc, bc, wa, wo):
    b, s, d = x.shape
    ch = u2.shape[2]
    tm = 256
    gcol = off_g // d
    tok = lambda w: pl.BlockSpec((1, tm, w), lambda bi, i: (bi, i, 0))
    full = lambda a: pl.BlockSpec(a.shape, lambda bi, i: (0,) * a.ndim, pipeline_mode=pl.Buffered(1))

    def grp(a):
        if a.shape[1] == 1:
            return a.reshape(b, s, GROUP_WIDTH), tok(GROUP_WIDTH)
        dil = a.shape[1]
        return a, pl.BlockSpec((1, dil, tm // dil, GROUP_WIDTH), lambda bi, i: (bi, 0, i, 0))

    grp_args, grp_specs = zip(*[grp(a) for a in list(os_) + list(ls_)])
    n2 = norm2_g.reshape(1, d)
    bc2 = bc.reshape(1, d)
    return pl.pallas_call(
        _post_kernel,
        grid=(b, s // tm),
        in_specs=[tok(ch)] + list(grp_specs)
                 + [pl.BlockSpec((1, tm, d), lambda bi, i: (bi, i, gcol)),
                    pl.BlockSpec((1, tm, d), lambda bi, i: (bi, i, gcol + 1)),
                    tok(d),
                    pl.BlockSpec((1, 6, d), lambda bi, i: (bi, 0, 0)),
                    full(n2), full(wc), full(bc2), full(wa), full(wo)],
        out_specs=[tok(d), tok(d)],
        out_shape=[jax.ShapeDtypeStruct((b, s, d), F32), jax.ShapeDtypeStruct((b, s, d), BF16)],
        scratch_shapes=[pltpu.VMEM((GROUP_WIDTH // LANES, tm, LANES), F32)] * 4,
        compiler_params=_params("parallel", "parallel"),
        name="merge_out_proj",
    )(u2, *grp_args, z3, z3, x, mod3, n2, wc, bc2, wa, wo)


def _peer_scores_kernel(h_ref, wq_ref, keys_ref, s_ref):
    qp = jnp.dot(h_ref[...], wq_ref[...], preferred_element_type=F32).astype(BF16)
    nt = (((1,), (1,)), ((), ()))
    half = keys_ref.shape[2]
    for hp in range(keys_ref.shape[0]):
        s_ref[hp] = lax.dot_general(keys_ref[hp], qp[:, hp * half:(hp + 1) * half], nt,
                                    preferred_element_type=F32)


def _peer_scores(h2, wq, keys):
    t, d = h2.shape
    nhp, nk, half = keys.shape
    tm = 256
    return pl.pallas_call(
        _peer_scores_kernel,
        grid=(t // tm,),
        in_specs=[pl.BlockSpec((tm, d), lambda i: (i, 0)),
                  pl.BlockSpec(wq.shape, lambda i: (0, 0), pipeline_mode=pl.Buffered(1)),
                  pl.BlockSpec(keys.shape, lambda i: (0, 0, 0), pipeline_mode=pl.Buffered(1))],
        out_specs=pl.BlockSpec((nhp, nk, tm), lambda i: (0, 0, i)),
        out_shape=jax.ShapeDtypeStruct((nhp, nk, t), F32),
        compiler_params=_params("parallel"),
        name="peer_scores",
    )(h2, wq, keys)


def _top16(s, tie_safe):
    key = lax.broadcasted_iota(jnp.int32, s.shape, 0).astype(F32)
    slot = lax.broadcasted_iota(jnp.int32, (PEER_TOPK, s.shape[1]), 0)
    rank = jnp.full(s.shape, float(PEER_TOPK), F32)
    vals = jnp.zeros((PEER_TOPK, s.shape[1]), F32)
    for k in range(PEER_TOPK):
        m = jnp.max(s, axis=0, keepdims=True)
        sel = s == m
        if tie_safe:
            sel = key == jnp.min(jnp.where(sel, key, float(PEER_NKEYS)), axis=0, keepdims=True)
        rank = jnp.where(sel, float(k), rank)
        s = jnp.where(sel, -jnp.inf, s)
        vals = jnp.where(slot == k, m, vals)
    return vals, rank


def _ranks_are_a_top16(rank):
    expect = float(PEER_NKEYS * PEER_TOPK - PEER_TOPK * (PEER_TOPK + 1) // 2)
    return jnp.sum(rank, axis=0, keepdims=True) == expect


def _staircase_lengths(v1, v2, tie_safe):
    tl = v1.shape[1]
    j8 = lax.broadcasted_iota(jnp.int32, (8, tl), 0)
    j16 = lax.broadcasted_iota(jnp.int32, (PEER_TOPK, tl), 0)
    cands = [v1[0:1, :] + v2]
    flats = [j16]
    for i in range(1, 8):
        cands.append(jnp.where(j8 < PEER_TOPK // (i + 1), v1[i:i + 1, :] + v2[0:8, :], -jnp.inf))
        flats.append(j8 + i * PEER_TOPK)
    cands.append(v1[8:16, :] + v2[0:1, :])
    flats.append((j8 + 8) * PEER_TOPK)
    cand = jnp.concatenate(cands, axis=0)
    flat = jnp.concatenate(flats, axis=0).astype(F32)
    big = float(PEER_TOPK * PEER_TOPK)
    in_staircase = cand > -jnp.inf
    z = jnp.zeros((1, tl), F32)
    top0 = None
    for k in range(PEER_TOPK):
        m = jnp.max(cand, axis=0, keepdims=True)
        sel = cand == m
        if tie_safe:
            sel = flat == jnp.min(jnp.where(sel, flat, big), axis=0, keepdims=True)
        cand = jnp.where(sel, -jnp.inf, cand)
        if k == 0:
            top0 = m
        z = z + jnp.exp(m - top0)
    picked = jnp.where(jnp.logical_and(in_staircase, cand == -jnp.inf), 1.0, 0.0)
    ok = jnp.sum(picked, axis=0, keepdims=True) == float(PEER_TOPK)
    lens_lo = jnp.zeros((8, tl), F32)
    lens_lo = jnp.where(j8 == 0, jnp.sum(picked[0:16], axis=0, keepdims=True), lens_lo)
    for i in range(1, 8):
        lens_lo = jnp.where(j8 == i, jnp.sum(picked[8 + 8 * i:16 + 8 * i], axis=0, keepdims=True), lens_lo)
    return jnp.concatenate([lens_lo, picked[72:80]], axis=0), z, ok


SELECT_HEADS_PER_BLOCK = 8


def _select_heads(s_ref, e1_ref, len_ref, r2_ref, e2_ref, h0, tie_safe):
    ok = None
    for dh in range(SELECT_HEADS_PER_BLOCK):
        h = h0 + dh
        s1 = s_ref[2 * h]
        s2 = s_ref[2 * h + 1]
        v1, rank1 = _top16(s1, tie_safe)
        v2, rank2 = _top16(s2, tie_safe)
        lens, z, good = _staircase_lengths(v1, v2, tie_safe)
        for rank in (rank1, rank2):
            good = jnp.logical_and(good, _ranks_are_a_top16(rank))
        ok = good if ok is None else jnp.logical_and(ok, good)
        sel_len = jnp.zeros(rank1.shape, F32)
        for i in range(PEER_TOPK):
            sel_len = jnp.where(rank1 == float(i), lens[i:i + 1, :], sel_len)
        e1_ref[h] = jnp.exp(s1 - v1[0:1, :])
        len_ref[h] = sel_len
        r2_ref[h] = rank2.astype(r2_ref.dtype)
        e2_ref[h] = (jnp.exp(s2 - v2[0:1, :]) / z).astype(e2_ref.dtype)
    return ok


def _peer_select_kernel(s_ref, e1_ref, len_ref, r2_ref, e2_ref):
    def block(i, carry):
        h0 = i * SELECT_HEADS_PER_BLOCK
        ok = _select_heads(s_ref, e1_ref, len_ref, r2_ref, e2_ref, h0, tie_safe=False)

        @pl.when(jnp.sum(jnp.where(ok, 0.0, 1.0)) > 0.0)
        def _():
            _select_heads(s_ref, e1_ref, len_ref, r2_ref, e2_ref, h0, tie_safe=True)

        return carry

    lax.fori_loop(0, PEER_HEADS // SELECT_HEADS_PER_BLOCK, block, 0)


def _peer_select(scores_t):
    nhp, nk, t = scores_t.shape
    tl = 128
    out_spec = pl.BlockSpec((PEER_HEADS, nk, tl), lambda i: (0, 0, i))
    shp = lambda dt: jax.ShapeDtypeStruct((PEER_HEADS, nk, t), dt)
    return pl.pallas_call(
        _peer_select_kernel,
        grid=(t // tl,),
        in_specs=[pl.BlockSpec((nhp, nk, tl), lambda i: (0, 0, i))],
        out_specs=[out_spec] * 4,
        out_shape=[shp(F32), shp(F32), shp(BF16), shp(BF16)],
        compiler_params=_params("parallel"),
        name="peer_select",
    )(scores_t)


def _peer_dense_kernel(h_ref, wu_ref, wdt_ref, e1_ref, len_ref, r2_ref, e2_ref, x1_ref, mod_ref,
                       o_ref, acc_ref):
    j = pl.program_id(1)

    @pl.when(j == 0)
    def _():
        acc_ref[...] = jnp.zeros_like(acc_ref)

    nt = (((1,), (1,)), ((), ()))
    pre = lax.dot_general(wu_ref[...], h_ref[...], nt, preferred_element_type=F32)
    act = (0.5 * pre * (1.0 + lax.erf(pre * (0.5 ** 0.5)))).astype(BF16)
    nk = r2_ref.shape[1]
    parts = []
    for c in range(wu_ref.shape[0] // nk):
        gate = jnp.zeros((nk, h_ref.shape[0]), BF16)
        for h in range(PEER_HEADS):
            sel_len = len_ref[h, c:c + 1, :].astype(BF16)
            e1 = e1_ref[h, c:c + 1, :].astype(BF16)
            gate = gate + jnp.where(r2_ref[h] < sel_len, e2_ref[h], jnp.zeros((), BF16)) * e1
        parts.append(act[c * nk:(c + 1) * nk, :] * gate)
    p = jnp.concatenate(parts, axis=0)
    acc_ref[...] += jnp.dot(wdt_ref[...], p, preferred_element_type=F32)

    @pl.when(j == pl.num_programs(1) - 1)
    def _():
        o_ref[...] = x1_ref[...] + mod_ref[0, 5:6, :] * acc_ref[...].T


def _peer_dense(h2, wu, wdt, e1t, lent, r2t, e2t, x1, mod3):
    t, d = h2.shape
    ne = wu.shape[0]
    nk = r2t.shape[1]
    tl = 512
    te = 8 * nk
    small = pl.BlockSpec((PEER_HEADS, 8, tl), lambda i, j: (0, j, i))
    table = pl.BlockSpec((PEER_HEADS, nk, tl), lambda i, j: (0, 0, i))
    per_b = t // mod3.shape[0] // tl
    once = pl.Buffered(1)
    return pl.pallas_call(
        _peer_dense_kernel,
        grid=(t // tl, ne // te),
        in_specs=[pl.BlockSpec((tl, d), lambda i, j: (i, 0), pipeline_mode=once),
                  pl.BlockSpec((te, d), lambda i, j: (j, 0)),
                  pl.BlockSpec((d, te), lambda i, j: (0, j)),
                  small, small, table, table,
                  pl.BlockSpec((tl, d), lambda i, j: (i, 0), pipeline_mode=once),
                  pl.BlockSpec((1, 6, d), lambda i, j: (i // per_b, 0, 0))],
        out_specs=pl.BlockSpec((tl, d), lambda i, j: (i, 0)),
        out_shape=jax.ShapeDtypeStruct((t, d), F32),
        scratch_shapes=[pltpu.VMEM((d, tl), F32)],
        compiler_params=_params("parallel", "arbitrary"),
        name="peer_dense",
    )(h2, wu, wdt, e1t, lent, r2t, e2t, x1, mod3)


def kernel(x, c, norm1_g, norm2_g, w_ada, b_ada, w_in, conv_dw, conv_db, conv_ln_g, conv_ln_b,
           w_conv_out, b_conv_out, q_norm_g, k_norm_g, w_attn_o, w_out,
           peer_w_q, peer_sub_keys, peer_w_up, peer_w_down):
    b, s, d = x.shape
    t = b * s
    depth = w_ada.shape[0]
    conv_ch = conv_dw.shape[2]
    in_cols = w_in.shape[2]
    tn = GROUP_WIDTH
    ngrp = len(DILATIONS)
    glu_t = 2 * conv_ch // tn
    qkv_t = lambda gi: [glu_t + gi, glu_t + ngrp + gi, glu_t + 2 * ngrp + gi]
    gate_t = list(range(glu_t + 3 * ngrp, in_cols // tn))
    main_t = list(range(glu_t)) + gate_t + qkv_t(0)
    off_g = glu_t * tn
    for l in range(depth):
        mod3 = _ada(c, w_ada[l], b_ada[l]).reshape(b, 6, d)
        hs = _prenorm(x, norm1_g[l], mod3)
        z = _in_proj(hs[0].reshape(t, d), w_in[l], IN_PROJ_ROWS, tn, main_t, "in_proj")
        z3 = z.reshape(b, s, len(main_t) * tn)
        u2 = _conv_branch(z3, conv_dw[l], conv_db[l], conv_ln_g[l], conv_ln_b[l])
        q_g = q_norm_g[l].reshape(ngrp, 1, GROUP_WIDTH)
        k_g = k_norm_g[l].reshape(ngrp, 1, GROUP_WIDTH)
        outs, lses = [], []
        for gi, dil in enumerate(DILATIONS):
            if dil == 1:
                zg, cq = z3, len(main_t) - 3
            else:
                zg = _in_proj(hs[gi].reshape(t, d), w_in[l], IN_PROJ_ROWS, tn, qkv_t(gi), f"in_proj_dil{dil}")
                zg, cq = zg.reshape(b * dil, s // dil, 3 * tn), 0
            o, lse = _attn_group(zg, q_g[gi], k_g[gi], dil, cq, cq + 1, cq + 2)
            outs.append(o.reshape(b, dil, s // dil, GROUP_WIDTH))
            lses.append(lse.reshape(b, dil, s // dil, GROUP_WIDTH))
        x1, h2 = _post(u2, outs, lses, z3, off_g, x, mod3, norm2_g[l],
                       w_conv_out[l].astype(BF16), b_conv_out[l], w_attn_o[l].astype(BF16),
                       w_out[l].astype(BF16))
        h2 = h2.reshape(t, d)
        keys = peer_sub_keys[l].reshape(2 * PEER_HEADS, PEER_NKEYS, -1).astype(BF16)
        scores_t = _peer_scores(h2, peer_w_q[l].astype(BF16), keys)
        e1t, lent, r2t, e2t = _peer_select(scores_t)
        x = _peer_dense(h2, peer_w_up[l].astype(BF16), peer_w_down[l].T.astype(BF16),
                        e1t, lent, r2t, e2t, x1.reshape(t, d), mod3).reshape(b, s, d)
    return x
```

```python
import functools

import jax
import jax.numpy as jnp
from jax import lax
from jax.experimental import pallas as pl
from jax.experimental.pallas import tpu as pltpu

F32 = jnp.float32
BF16 = jnp.bfloat16

EPS = 1e-6
CONV_K = 31
HEAD_DIM = 128
HEADS_PER_GROUP = 4
GROUP_WIDTH = HEADS_PER_GROUP * HEAD_DIM
DILATIONS = (1, 4, 16)
BAND = 128
PEER_HEADS = 8
PEER_NKEYS = 128
PEER_TOPK = 16
HALO = 32
LANES = 128
SUBLANES = 8

VMEM_LIMIT = 56 * 1024 * 1024


def _params(*sem):
    return pltpu.CompilerParams(dimension_semantics=sem, vmem_limit_bytes=VMEM_LIMIT)


def _ada_kernel(ct_ref, w_ref, b_ref, o_ref):
    d, tn = w_ref.shape
    nb = ct_ref.shape[1]
    rows = 8

    def body(i, accs):
        r0 = pl.multiple_of(i * rows, rows)
        w8 = w_ref[pl.ds(r0, rows), :]
        c8 = ct_ref[pl.ds(r0, rows), :]
        c8 = c8 * jax.nn.sigmoid(c8)
        return tuple(a + w8 * c8[:, b:b + 1] for b, a in enumerate(accs))

    accs = lax.fori_loop(0, d // rows, body,
                         tuple(jnp.zeros((rows, tn), F32) for _ in range(nb)), unroll=8)
    out = jnp.concatenate([jnp.sum(a, axis=0, keepdims=True) for a in accs], axis=0)
    o_ref[...] = out + b_ref[...]


def _ada(c, w_ada, b_ada):
    nb, d = c.shape
    n = w_ada.shape[1]
    tn = 1024
    return pl.pallas_call(
        _ada_kernel,
        grid=(n // tn,),
        in_specs=[pl.BlockSpec((d, nb), lambda j: (0, 0)),
                  pl.BlockSpec((d, tn), lambda j: (0, j)),
                  pl.BlockSpec((1, tn), lambda j: (0, j))],
        out_specs=pl.BlockSpec((nb, tn), lambda j: (0, j)),
        out_shape=jax.ShapeDtypeStruct((nb, n), F32),
        compiler_params=_params("arbitrary"),
        name="ada_mod",
    )(c.T, w_ada, b_ada.reshape(1, n))


def _modulated_rmsnorm(x, g, shift, scale):
    y = x * lax.rsqrt(jnp.mean(x * x, axis=-1, keepdims=True) + EPS) * g
    return y * (1.0 + scale) + shift


def _prenorm_kernel(x_ref, g_ref, mod_ref, o_ref, *rest):
    perm_refs, h_ref = rest[:-1], rest[-1]
    ts = x_ref.shape[1]
    h = _modulated_rmsnorm(x_ref[0], g_ref[...], mod_ref[0, 0:1, :], mod_ref[0, 1:2, :])
    o_ref[0] = h.astype(o_ref.dtype)
    for c in range(h_ref.shape[0]):
        h_ref[c] = h[:, c * LANES:(c + 1) * LANES]
    for p_ref in perm_refs:
        dil = p_ref.shape[1]
        for r in range(dil):
            for c in range(h_ref.shape[0]):
                rows = h_ref[c, pl.ds(r, ts // dil, stride=dil), :]
                p_ref[0, r, :, c * LANES:(c + 1) * LANES] = rows.astype(p_ref.dtype)


def _prenorm(x, g, mod3):
    b, s, d = x.shape
    ts = 512
    dils = [dil for dil in DILATIONS if dil > 1]
    nat = pl.BlockSpec((1, ts, d), lambda bi, i: (bi, i, 0))
    return pl.pallas_call(
        _prenorm_kernel,
        grid=(b, s // ts),
        in_specs=[nat,
                  pl.BlockSpec((1, d), lambda bi, i: (0, 0)),
                  pl.BlockSpec((1, 6, d), lambda bi, i: (bi, 0, 0))],
        out_specs=[nat] + [pl.BlockSpec((1, dil, ts // dil, d), lambda bi, i: (bi, 0, i, 0)) for dil in dils],
        out_shape=[jax.ShapeDtypeStruct((b, s, d), BF16)]
                  + [jax.ShapeDtypeStruct((b, dil, s // dil, d), BF16) for dil in dils],
        scratch_shapes=[pltpu.VMEM((d // LANES, ts, LANES), F32)],
        compiler_params=_params("parallel", "parallel"),
        name="prenorm",
    )(x, g.reshape(1, d), mod3)


IN_PROJ_ROWS = 2048


def _matmul_kernel(cols_ref, a_ref, b_ref, o_ref):
    del cols_ref
    o_ref[...] = jnp.dot(a_ref[...], b_ref[...].astype(a_ref.dtype),
                         preferred_element_type=F32).astype(o_ref.dtype)


def _in_proj(a, b, tm, tn, col_tiles, name):
    m, k = a.shape
    nt = len(col_tiles)
    return pl.pallas_call(
        _matmul_kernel,
        grid_spec=pltpu.PrefetchScalarGridSpec(
            num_scalar_prefetch=1,
            grid=(m // tm, nt),
            in_specs=[pl.BlockSpec((tm, k), lambda i, j, cols: (i, 0)),
                      pl.BlockSpec((k, tn), lambda i, j, cols: (0, cols[j]))],
            out_specs=pl.BlockSpec((tm, tn), lambda i, j, cols: (i, j))),
        out_shape=jax.ShapeDtypeStruct((m, nt * tn), BF16),
        compiler_params=_params("parallel", "arbitrary"),
        name=name,
    )(jnp.asarray(col_tiles, jnp.int32), a, b)


def _conv_kernel(ac_ref, gc_ref, ap_ref, gp_ref, dw_ref, db_ref, lg_ref, lb_ref, o_ref, buf_ref, shift_ref):
    ts = ac_ref.shape[1]
    i = pl.program_id(1)

    def glu(a, g):
        return a.astype(F32) * jax.nn.sigmoid(g.astype(F32))

    halo = glu(ap_ref[0], gp_ref[0])
    buf_ref[0:HALO, :] = jnp.where(i > 0, halo, 0.0)
    buf_ref[HALO:, :] = glu(ac_ref[0], gc_ref[0])
    span = shift_ref.shape[1]
    for b in range(1, SUBLANES):
        shift_ref[b - 1] = buf_ref[b:b + span, :]
    rows = 32
    first = HALO - (CONV_K - 1)
    for r in range(ts // rows):
        acc = jnp.zeros((rows, ac_ref.shape[2]), F32)
        for j in range(CONV_K):
            b = (first + j) % SUBLANES
            lo = r * rows + first + j - b
            window = buf_ref[lo:lo + rows, :] if b == 0 else shift_ref[b - 1, lo:lo + rows, :]
            acc = acc + dw_ref[j:j + 1, :] * window
        acc = acc + db_ref[...]
        mu = jnp.mean(acc, axis=-1, keepdims=True)
        cen = acc - mu
        var = jnp.mean(cen * cen, axis=-1, keepdims=True)
        y = cen * lax.rsqrt(var + EPS) * lg_ref[...] + lb_ref[...]
        o_ref[0, r * rows:(r + 1) * rows, :] = (y * jax.nn.sigmoid(y)).astype(o_ref.dtype)


def _conv_branch(z3, conv_dw, conv_db, ln_g, ln_b):
    b, s, _ = z3.shape
    ch = conv_dw.shape[1]
    ts = 512
    per = ts // HALO
    prev = lambda col: (lambda bi, i: (bi, jnp.maximum(i * per - 1, 0), col))
    vec = pl.BlockSpec((1, ch), lambda bi, i: (0, 0))
    return pl.pallas_call(
        _conv_kernel,
        grid=(b, s // ts),
        in_specs=[pl.BlockSpec((1, ts, ch), lambda bi, i: (bi, i, 0)),
                  pl.BlockSpec((1, ts, ch), lambda bi, i: (bi, i, 1)),
                  pl.BlockSpec((1, HALO, ch), prev(0)),
                  pl.BlockSpec((1, HALO, ch), prev(1)),
                  pl.BlockSpec((CONV_K, ch), lambda bi, i: (0, 0)),
                  vec, vec, vec],
        out_specs=pl.BlockSpec((1, ts, ch), lambda bi, i: (bi, i, 0)),
        out_shape=jax.ShapeDtypeStruct((b, s, ch), BF16),
        scratch_shapes=[pltpu.VMEM((ts + HALO, ch), F32),
                        pltpu.VMEM((SUBLANES - 1, ts + HALO - SUBLANES, ch), F32)],
        compiler_params=_params("parallel", "parallel"),
        name="conv_branch",
    )(z3, z3, z3, z3, conv_dw, conv_db.reshape(1, ch), ln_g.reshape(1, ch), ln_b.reshape(1, ch))


def _attn_kernel(q_ref, k_ref, v_ref, kh_ref, vh_ref, qg_ref, kg_ref, o_ref, l_ref):
    n = pl.program_id(1)
    qi = lax.broadcasted_iota(jnp.int32, (BAND, BAND), 0)
    ki = lax.broadcasted_iota(jnp.int32, (BAND, BAND), 1)
    mask_cur = ki <= qi
    mask_prev = ki >= qi
    mask_halo = ki >= qi + jnp.where(n > 0, 0, 2 * BAND)

    def headnorm(t, g):
        t = t.astype(F32)
        return t * lax.rsqrt(jnp.mean(t * t, axis=-1, keepdims=True) + EPS) * g

    nt = (((1,), (1,)), ((), ()))
    for j in range(HEADS_PER_GROUP):
        sl = slice(j * HEAD_DIM, (j + 1) * HEAD_DIM)
        qn = (headnorm(q_ref[0, :, sl], qg_ref[:, sl]) * (HEAD_DIM ** -0.5)).astype(BF16)
        kn = headnorm(k_ref[0, :, sl], kg_ref[:, sl]).astype(BF16)
        kh = headnorm(kh_ref[0, :, sl], kg_ref[:, sl]).astype(BF16)
        for blk in range(q_ref.shape[1] // BAND):
            rows = slice(blk * BAND, (blk + 1) * BAND)
            back = slice((blk - 1) * BAND, blk * BAND)
            q, kc, vc = qn[rows], kn[rows], v_ref[0, rows, sl]
            if blk == 0:
                kp, vp, mp = kh, vh_ref[0, :, sl], mask_halo
            else:
                kp, vp, mp = kn[back], v_ref[0, back, sl], mask_prev
            sp = jnp.where(mp, lax.dot_general(q, kp, nt, preferred_element_type=F32), -jnp.inf)
            sc = jnp.where(mask_cur, lax.dot_general(q, kc, nt, preferred_element_type=F32), -jnp.inf)
            m = jnp.maximum(jnp.max(sp, axis=-1, keepdims=True), jnp.max(sc, axis=-1, keepdims=True))
            pp = jnp.exp(sp - m)
            pc = jnp.exp(sc - m)
            den = jnp.sum(pp, axis=-1, keepdims=True) + jnp.sum(pc, axis=-1, keepdims=True)
            o = (jnp.dot(pp.astype(BF16), vp, preferred_element_type=F32)
                 + jnp.dot(pc.astype(BF16), vc, preferred_element_type=F32))
            o_ref[0, rows, sl] = (o / den).astype(o_ref.dtype)
            l_ref[0, rows, sl] = jnp.broadcast_to(m + jnp.log(den), (BAND, HEAD_DIM))


ATTN_TILE_ROWS = 512


def _attn_group(zg, q_g, k_g, dil, cq, ck, cv):
    nseq, l, _ = zg.shape
    rows = min(l, ATTN_TILE_ROWS)
    per = rows // BAND
    tile = lambda c: pl.BlockSpec((1, rows, GROUP_WIDTH), lambda si, n: (si, n, c))
    halo = lambda c: pl.BlockSpec((1, BAND, GROUP_WIDTH), lambda si, n: (si, jnp.maximum(n * per - 1, 0), c))
    gain = pl.BlockSpec((1, GROUP_WIDTH), lambda si, n: (0, 0))
    out_spec = tile(0)
    return pl.pallas_call(
        _attn_kernel,
        grid=(nseq, l // rows),
        in_specs=[tile(cq), tile(ck), tile(cv), halo(ck), halo(cv), gain, gain],
        out_specs=[out_spec, out_spec],
        out_shape=[jax.ShapeDtypeStruct((nseq, l, GROUP_WIDTH), BF16),
                   jax.ShapeDtypeStruct((nseq, l, GROUP_WIDTH), F32)],
        compiler_params=_params("parallel", "arbitrary"),
        name=f"dilated_attn_{dil}",
    )(zg, zg, zg, zg, zg, q_g, k_g)


def _post_kernel(u_ref, o0_ref, o1_ref, o2_ref, l0_ref, l1_ref, l2_ref, ga_ref, gb_ref, x_ref,
                 mod_ref, n2_ref, wc_ref, bc_ref, wa_ref, wo_ref, x1_ref, h2_ref,
                 so1_ref, sl1_ref, so2_ref, sl2_ref):
    yc = jnp.dot(u_ref[0], wc_ref[...], preferred_element_type=F32) + bc_ref[...]

    def sequence_order(src_ref, dst_ref):
        dil, rows = src_ref.shape[1], src_ref.shape[2]
        nch = dst_ref.shape[0]
        for r in range(dil):
            src = src_ref[0, r].astype(F32)
            for c in range(nch):
                dst_ref[c, pl.ds(r, rows, stride=dil), :] = src[:, c * LANES:(c + 1) * LANES]
        return jnp.concatenate([dst_ref[c] for c in range(nch)], axis=1)

    o0, l0 = o0_ref[0].astype(F32), l0_ref[0]
    o1, l1 = sequence_order(o1_ref, so1_ref), sequence_order(l1_ref, sl1_ref)
    o2, l2 = sequence_order(o2_ref, so2_ref), sequence_order(l2_ref, sl2_ref)
    lm = jnp.maximum(jnp.maximum(l0, l1), l2)
    e0, e1, e2 = jnp.exp(l0 - lm), jnp.exp(l1 - lm), jnp.exp(l2 - lm)
    o = (e0 * o0 + e1 * o1 + e2 * o2) / (e0 + e1 + e2)
    ya = jnp.dot(o.astype(BF16), wa_ref[...], preferred_element_type=F32)
    merged = (jax.nn.sigmoid(ga_ref[0].astype(F32)) * yc + jax.nn.sigmoid(gb_ref[0].astype(F32)) * ya)
    out = jnp.dot(merged.astype(BF16), wo_ref[...], preferred_element_type=F32)
    x1 = x_ref[0] + mod_ref[0, 2:3, :] * out
    x1_ref[0] = x1
    h2 = _modulated_rmsnorm(x1, n2_ref[...], mod_ref[0, 3:4, :], mod_ref[0, 4:5, :])
    h2_ref[0] = h2.astype(h2_ref.dtype)


def _post(u2, os_, ls_, z3, off_g, x, mod3, norm2_g, wc, bc, wa, wo):
    b, s, d = x.shape
    ch = u2.shape[2]
    tm = 256
    gcol = off_g // d
    tok = lambda w: pl.BlockSpec((1, tm, w), lambda bi, i: (bi, i, 0))
    full = lambda a: pl.BlockSpec(a.shape, lambda bi, i: (0,) * a.ndim, pipeline_mode=pl.Buffered(1))

    def grp(a):
        if a.shape[1] == 1:
            return a.reshape(b, s, GROUP_WIDTH), tok(GROUP_WIDTH)
        dil = a.shape[1]
        return a, pl.BlockSpec((1, dil, tm // dil, GROUP_WIDTH), lambda bi, i: (bi, 0, i, 0))

    grp_args, grp_specs = zip(*[grp(a) for a in list(os_) + list(ls_)])
    n2 = norm2_g.reshape(1, d)
    bc2 = bc.reshape(1, d)
    return pl.pallas_call(
        _post_kernel,
        grid=(b, s // tm),
        in_specs=[tok(ch)] + list(grp_specs)
                 + [pl.BlockSpec((1, tm, d), lambda bi, i: (bi, i, gcol)),
                    pl.BlockSpec((1, tm, d), lambda bi, i: (bi, i, gcol + 1)),
                    tok(d),
                    pl.BlockSpec((1, 6, d), lambda bi, i: (bi, 0, 0)),
                    full(n2), full(wc), full(bc2), full(wa), full(wo)],
        out_specs=[tok(d), tok(d)],
        out_shape=[jax.ShapeDtypeStruct((b, s, d), F32), jax.ShapeDtypeStruct((b, s, d), BF16)],
        scratch_shapes=[pltpu.VMEM((GROUP_WIDTH // LANES, tm, LANES), F32)] * 4,
        compiler_params=_params("parallel", "parallel"),
        name="merge_out_proj",
    )(u2, *grp_args, z3, z3, x, mod3, n2, wc, bc2, wa, wo)


def _peer_scores_kernel(h_ref, wq_ref, keys_ref, s_ref):
    qp = jnp.dot(h_ref[...], wq_ref[...], preferred_element_type=F32).astype(BF16)
    nt = (((1,), (1,)), ((), ()))
    half = keys_ref.shape[2]
    for hp in range(keys_ref.shape[0]):
        s_ref[hp] = lax.dot_general(keys_ref[hp], qp[:, hp * half:(hp + 1) * half], nt,
                                    preferred_element_type=F32)


def _peer_scores(h2, wq, keys):
    t, d = h2.shape
    nhp, nk, half = keys.shape
    tm = 256
    return pl.pallas_call(
        _peer_scores_kernel,
        grid=(t // tm,),
        in_specs=[pl.BlockSpec((tm, d), lambda i: (i, 0)),
                  pl.BlockSpec(wq.shape, lambda i: (0, 0), pipeline_mode=pl.Buffered(1)),
                  pl.BlockSpec(keys.shape, lambda i: (0, 0, 0), pipeline_mode=pl.Buffered(1))],
        out_specs=pl.BlockSpec((nhp, nk, tm), lambda i: (0, 0, i)),
        out_shape=jax.ShapeDtypeStruct((nhp, nk, t), F32),
        compiler_params=_params("parallel"),
        name="peer_scores",
    )(h2, wq, keys)


def _top16(s, tie_safe):
    key = lax.broadcasted_iota(jnp.int32, s.shape, 0).astype(F32)
    slot = lax.broadcasted_iota(jnp.int32, (PEER_TOPK, s.shape[1]), 0)
    rank = jnp.full(s.shape, float(PEER_TOPK), F32)
    vals = jnp.zeros((PEER_TOPK, s.shape[1]), F32)
    for k in range(PEER_TOPK):
        m = jnp.max(s, axis=0, keepdims=True)
        sel = s == m
        if tie_safe:
            sel = key == jnp.min(jnp.where(sel, key, float(PEER_NKEYS)), axis=0, keepdims=True)
        rank = jnp.where(sel, float(k), rank)
        s = jnp.where(sel, -jnp.inf, s)
        vals = jnp.where(slot == k, m, vals)
    return vals, rank


def _ranks_are_a_top16(rank):
    expect = float(PEER_NKEYS * PEER_TOPK - PEER_TOPK * (PEER_TOPK + 1) // 2)
    return jnp.sum(rank, axis=0, keepdims=True) == expect


def _staircase_lengths(v1, v2, tie_safe):
    tl = v1.shape[1]
    j8 = lax.broadcasted_iota(jnp.int32, (8, tl), 0)
    j16 = lax.broadcasted_iota(jnp.int32, (PEER_TOPK, tl), 0)
    cands = [v1[0:1, :] + v2]
    flats = [j16]
    for i in range(1, 8):
        cands.append(jnp.where(j8 < PEER_TOPK // (i + 1), v1[i:i + 1, :] + v2[0:8, :], -jnp.inf))
        flats.append(j8 + i * PEER_TOPK)
    cands.append(v1[8:16, :] + v2[0:1, :])
    flats.append((j8 + 8) * PEER_TOPK)
    cand = jnp.concatenate(cands, axis=0)
    flat = jnp.concatenate(flats, axis=0).astype(F32)
    big = float(PEER_TOPK * PEER_TOPK)
    in_staircase = cand > -jnp.inf
    z = jnp.zeros((1, tl), F32)
    top0 = None
    for k in range(PEER_TOPK):
        m = jnp.max(cand, axis=0, keepdims=True)
        sel = cand == m
        if tie_safe:
            sel = flat == jnp.min(jnp.where(sel, flat, big), axis=0, keepdims=True)
        cand = jnp.where(sel, -jnp.inf, cand)
        if k == 0:
            top0 = m
        z = z + jnp.exp(m - top0)
    picked = jnp.where(jnp.logical_and(in_staircase, cand == -jnp.inf), 1.0, 0.0)
    ok = jnp.sum(picked, axis=0, keepdims=True) == float(PEER_TOPK)
    lens_lo = jnp.zeros((8, tl), F32)
    lens_lo = jnp.where(j8 == 0, jnp.sum(picked[0:16], axis=0, keepdims=True), lens_lo)
    for i in range(1, 8):
        lens_lo = jnp.where(j8 == i, jnp.sum(picked[8 + 8 * i:16 + 8 * i], axis=0, keepdims=True), lens_lo)
    return jnp.concatenate([lens_lo, picked[72:80]], axis=0), z, ok


SELECT_HEADS_PER_BLOCK = 8


def _select_heads(s_ref, e1_ref, len_ref, r2_ref, e2_ref, h0, tie_safe):
    ok = None
    for dh in range(SELECT_HEADS_PER_BLOCK):
        h = h0 + dh
        s1 = s_ref[2 * h]
        s2 = s_ref[2 * h + 1]
        v1, rank1 = _top16(s1, tie_safe)
        v2, rank2 = _top16(s2, tie_safe)
        lens, z, good = _staircase_lengths(v1, v2, tie_safe)
        for rank in (rank1, rank2):
            good = jnp.logical_and(good, _ranks_are_a_top16(rank))
        ok = good if ok is None else jnp.logical_and(ok, good)
        sel_len = jnp.zeros(rank1.shape, F32)
        for i in range(PEER_TOPK):
            sel_len = jnp.where(rank1 == float(i), lens[i:i + 1, :], sel_len)
        e1_ref[h] = jnp.exp(s1 - v1[0:1, :])
        len_ref[h] = sel_len
        r2_ref[h] = rank2.astype(r2_ref.dtype)
        e2_ref[h] = (jnp.exp(s2 - v2[0:1, :]) / z).astype(e2_ref.dtype)
    return ok


def _peer_select_kernel(s_ref, e1_ref, len_ref, r2_ref, e2_ref):
    def block(i, carry):
        h0 = i * SELECT_HEADS_PER_BLOCK
        ok = _select_heads(s_ref, e1_ref, len_ref, r2_ref, e2_ref, h0, tie_safe=False)

        @pl.when(jnp.sum(jnp.where(ok, 0.0, 1.0)) > 0.0)
        def _():
            _select_heads(s_ref, e1_ref, len_ref, r2_ref, e2_ref, h0, tie_safe=True)

        return carry

    lax.fori_loop(0, PEER_HEADS // SELECT_HEADS_PER_BLOCK, block, 0)


def _peer_select(scores_t):
    nhp, nk, t = scores_t.shape
    tl = 128
    out_spec = pl.BlockSpec((PEER_HEADS, nk, tl), lambda i: (0, 0, i))
    shp = lambda dt: jax.ShapeDtypeStruct((PEER_HEADS, nk, t), dt)
    return pl.pallas_call(
        _peer_select_kernel,
        grid=(t // tl,),
        in_specs=[pl.BlockSpec((nhp, nk, tl), lambda i: (0, 0, i))],
        out_specs=[out_spec] * 4,
        out_shape=[shp(F32), shp(F32), shp(BF16), shp(BF16)],
        compiler_params=_params("parallel"),
        name="peer_select",
    )(scores_t)


def _peer_dense_kernel(h_ref, wu_ref, wdt_ref, e1_ref, len_ref, r2_ref, e2_ref, x1_ref, mod_ref,
                       o_ref, acc_ref):
    j = pl.program_id(1)

    @pl.when(j == 0)
    def _():
        acc_ref[...] = jnp.zeros_like(acc_ref)

    nt = (((1,), (1,)), ((), ()))
    pre = lax.dot_general(wu_ref[...], h_ref[...], nt, preferred_element_type=F32)
    act = (0.5 * pre * (1.0 + lax.erf(pre * (0.5 ** 0.5)))).astype(BF16)
    nk = r2_ref.shape[1]
    parts = []
    for c in range(wu_ref.shape[0] // nk):
        gate = jnp.zeros((nk, h_ref.shape[0]), BF16)
        for h in range(PEER_HEADS):
            sel_len = len_ref[h, c:c + 1, :].astype(BF16)
            e1 = e1_ref[h, c:c + 1, :].astype(BF16)
            gate = gate + jnp.where(r2_ref[h] < sel_len, e2_ref[h], jnp.zeros((), BF16)) * e1
        parts.append(act[c * nk:(c + 1) * nk, :] * gate)
    p = jnp.concatenate(parts, axis=0)
    tn = (((0,), (0,)), ((), ()))
    acc_ref[...] += lax.dot_general(wdt_ref[...], p, tn, preferred_element_type=F32)

    @pl.when(j == pl.num_programs(1) - 1)
    def _():
        o_ref[...] = x1_ref[...] + mod_ref[0, 5:6, :] * acc_ref[...].T


def _peer_dense(h2, wu, wdt, e1t, lent, r2t, e2t, x1, mod3):
    t, d = h2.shape
    ne = wu.shape[0]
    nk = r2t.shape[1]
    tl = 512
    te = 8 * nk
    small = pl.BlockSpec((PEER_HEADS, 8, tl), lambda i, j: (0, j, i))
    table = pl.BlockSpec((PEER_HEADS, nk, tl), lambda i, j: (0, 0, i))
    per_b = t // mod3.shape[0] // tl
    once = pl.Buffered(1)
    return pl.pallas_call(
        _peer_dense_kernel,
        grid=(t // tl, ne // te),
        in_specs=[pl.BlockSpec((tl, d), lambda i, j: (i, 0), pipeline_mode=once),
                  pl.BlockSpec((te, d), lambda i, j: (j, 0)),
                  pl.BlockSpec((te, d), lambda i, j: (j, 0)),
                  small, small, table, table,
                  pl.BlockSpec((tl, d), lambda i, j: (i, 0), pipeline_mode=once),
                  pl.BlockSpec((1, 6, d), lambda i, j: (i // per_b, 0, 0))],
        out_specs=pl.BlockSpec((tl, d), lambda i, j: (i, 0)),
        out_shape=jax.ShapeDtypeStruct((t, d), F32),
        scratch_shapes=[pltpu.VMEM((d, tl), F32)],
        compiler_params=_params("parallel", "arbitrary"),
        name="peer_dense",
    )(h2, wu, wdt, e1t, lent, r2t, e2t, x1, mod3)


def kernel(x, c, norm1_g, norm2_g, w_ada, b_ada, w_in, conv_dw, conv_db, conv_ln_g, conv_ln_b,
           w_conv_out, b_conv_out, q_norm_g, k_norm_g, w_attn_o, w_out,
           peer_w_q, peer_sub_keys, peer_w_up, peer_w_down):
    b, s, d = x.shape
    t = b * s
    depth = w_ada.shape[0]
    conv_ch = conv_dw.shape[2]
    in_cols = w_in.shape[2]
    tn = GROUP_WIDTH
    ngrp = len(DILATIONS)
    glu_t = 2 * conv_ch // tn
    qkv_t = lambda gi: [glu_t + gi, glu_t + ngrp + gi, glu_t + 2 * ngrp + gi]
    gate_t = list(range(glu_t + 3 * ngrp, in_cols // tn))
    main_t = list(range(glu_t)) + gate_t + qkv_t(0)
    off_g = glu_t * tn
    for l in range(depth):
        mod3 = _ada(c, w_ada[l], b_ada[l]).reshape(b, 6, d)
        hs = _prenorm(x, norm1_g[l], mod3)
        z = _in_proj(hs[0].reshape(t, d), w_in[l], IN_PROJ_ROWS, tn, main_t, "in_proj")
        z3 = z.reshape(b, s, len(main_t) * tn)
        u2 = _conv_branch(z3, conv_dw[l], conv_db[l], conv_ln_g[l], conv_ln_b[l])
        q_g = q_norm_g[l].reshape(ngrp, 1, GROUP_WIDTH)
        k_g = k_norm_g[l].reshape(ngrp, 1, GROUP_WIDTH)
        outs, lses = [], []
        for gi, dil in enumerate(DILATIONS):
            if dil == 1:
                zg, cq = z3, len(main_t) - 3
            else:
                zg = _in_proj(hs[gi].reshape(t, d), w_in[l], IN_PROJ_ROWS, tn, qkv_t(gi), f"in_proj_dil{dil}")
                zg, cq = zg.reshape(b * dil, s // dil, 3 * tn), 0
            o, lse = _attn_group(zg, q_g[gi], k_g[gi], dil, cq, cq + 1, cq + 2)
            outs.append(o.reshape(b, dil, s // dil, GROUP_WIDTH))
            lses.append(lse.reshape(b, dil, s // dil, GROUP_WIDTH))
        x1, h2 = _post(u2, outs, lses, z3, off_g, x, mod3, norm2_g[l],
                       w_conv_out[l].astype(BF16), b_conv_out[l], w_attn_o[l].astype(BF16),
                       w_out[l].astype(BF16))
        h2 = h2.reshape(t, d)
        keys = peer_sub_keys[l].reshape(2 * PEER_HEADS, PEER_NKEYS, -1).astype(BF16)
        scores_t = _peer_scores(h2, peer_w_q[l].astype(BF16), keys)
        e1t, lent, r2t, e2t = _peer_select(scores_t)
        x = _peer_dense(h2, peer_w_up[l].astype(BF16), peer_w_down[l].astype(BF16),
                        e1t, lent, r2t, e2t, x1.reshape(t, d), mod3).reshape(b, s, d)
    return x
```

```python
import functools

import jax
import jax.numpy as jnp
from jax import lax
from jax.experimental import pallas as pl
from jax.experimental.pallas import tpu as pltpu

F32 = jnp.float32
BF16 = jnp.bfloat16

EPS = 1e-6
CONV_K = 31
HEAD_DIM = 128
HEADS_PER_GROUP = 4
GROUP_WIDTH = HEADS_PER_GROUP * HEAD_DIM
DILATIONS = (1, 4, 16)
BAND = 128
PEER_HEADS = 8
PEER_NKEYS = 128
PEER_TOPK = 16
HALO = 32
LANES = 128
SUBLANES = 8

VMEM_LIMIT = 56 * 1024 * 1024


def _params(*sem):
    return pltpu.CompilerParams(dimension_semantics=sem, vmem_limit_bytes=VMEM_LIMIT)


def _ada_kernel(ct_ref, w_ref, b_ref, o_ref):
    d, tn = w_ref.shape
    nb = ct_ref.shape[1]
    rows = 8

    def body(i, accs):
        r0 = pl.multiple_of(i * rows, rows)
        w8 = w_ref[pl.ds(r0, rows), :]
        c8 = ct_ref[pl.ds(r0, rows), :]
        c8 = c8 * jax.nn.sigmoid(c8)
        return tuple(a + w8 * c8[:, b:b + 1] for b, a in enumerate(accs))

    accs = lax.fori_loop(0, d // rows, body,
                         tuple(jnp.zeros((rows, tn), F32) for _ in range(nb)), unroll=32)
    out = jnp.concatenate([jnp.sum(a, axis=0, keepdims=True) for a in accs], axis=0)
    o_ref[...] = out + b_ref[...]


def _ada(c, w_ada, b_ada):
    nb, d = c.shape
    n = w_ada.shape[1]
    tn = 1024
    return pl.pallas_call(
        _ada_kernel,
        grid=(n // tn,),
        in_specs=[pl.BlockSpec((d, nb), lambda j: (0, 0)),
                  pl.BlockSpec((d, tn), lambda j: (0, j)),
                  pl.BlockSpec((1, tn), lambda j: (0, j))],
        out_specs=pl.BlockSpec((nb, tn), lambda j: (0, j)),
        out_shape=jax.ShapeDtypeStruct((nb, n), F32),
        compiler_params=_params("arbitrary"),
        name="ada_mod",
    )(c.T, w_ada, b_ada.reshape(1, n))


def _modulated_rmsnorm(x, g, shift, scale):
    y = x * lax.rsqrt(jnp.mean(x * x, axis=-1, keepdims=True) + EPS) * g
    return y * (1.0 + scale) + shift


def _prenorm_kernel(x_ref, g_ref, mod_ref, o_ref, *rest):
    perm_refs, h_ref = rest[:-1], rest[-1]
    ts = x_ref.shape[1]
    h = _modulated_rmsnorm(x_ref[0], g_ref[...], mod_ref[0, 0:1, :], mod_ref[0, 1:2, :])
    o_ref[0] = h.astype(o_ref.dtype)
    for c in range(h_ref.shape[0]):
        h_ref[c] = h[:, c * LANES:(c + 1) * LANES]
    for p_ref in perm_refs:
        dil = p_ref.shape[1]
        for r in range(dil):
            for c in range(h_ref.shape[0]):
                rows = h_ref[c, pl.ds(r, ts // dil, stride=dil), :]
                p_ref[0, r, :, c * LANES:(c + 1) * LANES] = rows.astype(p_ref.dtype)


def _prenorm(x, g, mod3):
    b, s, d = x.shape
    ts = 512
    dils = [dil for dil in DILATIONS if dil > 1]
    nat = pl.BlockSpec((1, ts, d), lambda bi, i: (bi, i, 0))
    return pl.pallas_call(
        _prenorm_kernel,
        grid=(b, s // ts),
        in_specs=[nat,
                  pl.BlockSpec((1, d), lambda bi, i: (0, 0)),
                  pl.BlockSpec((1, 6, d), lambda bi, i: (bi, 0, 0))],
        out_specs=[nat] + [pl.BlockSpec((1, dil, ts // dil, d), lambda bi, i: (bi, 0, i, 0)) for dil in dils],
        out_shape=[jax.ShapeDtypeStruct((b, s, d), BF16)]
                  + [jax.ShapeDtypeStruct((b, dil, s // dil, d), BF16) for dil in dils],
        scratch_shapes=[pltpu.VMEM((d // LANES, ts, LANES), F32)],
        compiler_params=_params("parallel", "parallel"),
        name="prenorm",
    )(x, g.reshape(1, d), mod3)


IN_PROJ_ROWS = 2048


def _matmul_kernel(cols_ref, a_ref, b_ref, o_ref):
    del cols_ref
    o_ref[...] = jnp.dot(a_ref[...], b_ref[...].astype(a_ref.dtype),
                         preferred_element_type=F32).astype(o_ref.dtype)


def _in_proj(a, b, tm, tn, col_tiles, name):
    m, k = a.shape
    nt = len(col_tiles)
    return pl.pallas_call(
        _matmul_kernel,
        grid_spec=pltpu.PrefetchScalarGridSpec(
            num_scalar_prefetch=1,
            grid=(m // tm, nt),
            in_specs=[pl.BlockSpec((tm, k), lambda i, j, cols: (i, 0)),
                      pl.BlockSpec((k, tn), lambda i, j, cols: (0, cols[j]))],
            out_specs=pl.BlockSpec((tm, tn), lambda i, j, cols: (i, j))),
        out_shape=jax.ShapeDtypeStruct((m, nt * tn), BF16),
        compiler_params=_params("parallel", "arbitrary"),
        name=name,
    )(jnp.asarray(col_tiles, jnp.int32), a, b)


def _conv_kernel(ac_ref, gc_ref, ap_ref, gp_ref, dw_ref, db_ref, lg_ref, lb_ref, o_ref, buf_ref, shift_ref):
    ts = ac_ref.shape[1]
    i = pl.program_id(1)

    def glu(a, g):
        return a.astype(F32) * jax.nn.sigmoid(g.astype(F32))

    halo = glu(ap_ref[0], gp_ref[0])
    buf_ref[0:HALO, :] = jnp.where(i > 0, halo, 0.0)
    buf_ref[HALO:, :] = glu(ac_ref[0], gc_ref[0])
    span = shift_ref.shape[1]
    for b in range(1, SUBLANES):
        shift_ref[b - 1] = buf_ref[b:b + span, :]
    rows = 32
    first = HALO - (CONV_K - 1)
    for r in range(ts // rows):
        acc = jnp.zeros((rows, ac_ref.shape[2]), F32)
        for j in range(CONV_K):
            b = (first + j) % SUBLANES
            lo = r * rows + first + j - b
            window = buf_ref[lo:lo + rows, :] if b == 0 else shift_ref[b - 1, lo:lo + rows, :]
            acc = acc + dw_ref[j:j + 1, :] * window
        acc = acc + db_ref[...]
        mu = jnp.mean(acc, axis=-1, keepdims=True)
        cen = acc - mu
        var = jnp.mean(cen * cen, axis=-1, keepdims=True)
        y = cen * lax.rsqrt(var + EPS) * lg_ref[...] + lb_ref[...]
        o_ref[0, r * rows:(r + 1) * rows, :] = (y * jax.nn.sigmoid(y)).astype(o_ref.dtype)


def _conv_branch(z3, conv_dw, conv_db, ln_g, ln_b):
    b, s, _ = z3.shape
    ch = conv_dw.shape[1]
    ts = 256
    per = ts // HALO
    prev = lambda col: (lambda bi, i: (bi, jnp.maximum(i * per - 1, 0), col))
    vec = pl.BlockSpec((1, ch), lambda bi, i: (0, 0))
    return pl.pallas_call(
        _conv_kernel,
        grid=(b, s // ts),
        in_specs=[pl.BlockSpec((1, ts, ch), lambda bi, i: (bi, i, 0)),
                  pl.BlockSpec((1, ts, ch), lambda bi, i: (bi, i, 1)),
                  pl.BlockSpec((1, HALO, ch), prev(0)),
                  pl.BlockSpec((1, HALO, ch), prev(1)),
                  pl.BlockSpec((CONV_K, ch), lambda bi, i: (0, 0)),
                  vec, vec, vec],
        out_specs=pl.BlockSpec((1, ts, ch), lambda bi, i: (bi, i, 0)),
        out_shape=jax.ShapeDtypeStruct((b, s, ch), BF16),
        scratch_shapes=[pltpu.VMEM((ts + HALO, ch), F32),
                        pltpu.VMEM((SUBLANES - 1, ts + HALO - SUBLANES, ch), F32)],
        compiler_params=_params("parallel", "parallel"),
        name="conv_branch",
    )(z3, z3, z3, z3, conv_dw, conv_db.reshape(1, ch), ln_g.reshape(1, ch), ln_b.reshape(1, ch))


def _attn_kernel(q_ref, k_ref, v_ref, kh_ref, vh_ref, qg_ref, kg_ref, o_ref, l_ref):
    n = pl.program_id(1)
    qi = lax.broadcasted_iota(jnp.int32, (BAND, BAND), 0)
    ki = lax.broadcasted_iota(jnp.int32, (BAND, BAND), 1)
    mask_cur = ki <= qi
    mask_prev = ki >= qi
    mask_halo = ki >= qi + jnp.where(n > 0, 0, 2 * BAND)

    def headnorm(t, g):
        t = t.astype(F32)
        return t * lax.rsqrt(jnp.mean(t * t, axis=-1, keepdims=True) + EPS) * g

    nt = (((1,), (1,)), ((), ()))
    for j in range(HEADS_PER_GROUP):
        sl = slice(j * HEAD_DIM, (j + 1) * HEAD_DIM)
        qn = (headnorm(q_ref[0, :, sl], qg_ref[:, sl]) * (HEAD_DIM ** -0.5)).astype(BF16)
        kn = headnorm(k_ref[0, :, sl], kg_ref[:, sl]).astype(BF16)
        kh = headnorm(kh_ref[0, :, sl], kg_ref[:, sl]).astype(BF16)
        for blk in range(q_ref.shape[1] // BAND):
            rows = slice(blk * BAND, (blk + 1) * BAND)
            back = slice((blk - 1) * BAND, blk * BAND)
            q, kc, vc = qn[rows], kn[rows], v_ref[0, rows, sl]
            if blk == 0:
                kp, vp, mp = kh, vh_ref[0, :, sl], mask_halo
            else:
                kp, vp, mp = kn[back], v_ref[0, back, sl], mask_prev
            sp = jnp.where(mp, lax.dot_general(q, kp, nt, preferred_element_type=F32), -jnp.inf)
            sc = jnp.where(mask_cur, lax.dot_general(q, kc, nt, preferred_element_type=F32), -jnp.inf)
            m = jnp.maximum(jnp.max(sp, axis=-1, keepdims=True), jnp.max(sc, axis=-1, keepdims=True))
            pp = jnp.exp(sp - m)
            pc = jnp.exp(sc - m)
            den = jnp.sum(pp, axis=-1, keepdims=True) + jnp.sum(pc, axis=-1, keepdims=True)
            o = (jnp.dot(pp.astype(BF16), vp, preferred_element_type=F32)
                 + jnp.dot(pc.astype(BF16), vc, preferred_element_type=F32))
            o_ref[0, rows, sl] = (o / den).astype(o_ref.dtype)
            l_ref[0, rows, sl] = jnp.broadcast_to(m + jnp.log(den), (BAND, HEAD_DIM))


ATTN_TILE_ROWS = 512


def _attn_group(zg, q_g, k_g, dil, cq, ck, cv):
    nseq, l, _ = zg.shape
    rows = min(l, ATTN_TILE_ROWS)
    per = rows // BAND
    tile = lambda c: pl.BlockSpec((1, rows, GROUP_WIDTH), lambda si, n: (si, n, c))
    halo = lambda c: pl.BlockSpec((1, BAND, GROUP_WIDTH), lambda si, n: (si, jnp.maximum(n * per - 1, 0), c))
    gain = pl.BlockSpec((1, GROUP_WIDTH), lambda si, n: (0, 0))
    out_spec = tile(0)
    return pl.pallas_call(
        _attn_kernel,
        grid=(nseq, l // rows),
        in_specs=[tile(cq), tile(ck), tile(cv), halo(ck), halo(cv), gain, gain],
        out_specs=[out_spec, out_spec],
        out_shape=[jax.ShapeDtypeStruct((nseq, l, GROUP_WIDTH), BF16),
                   jax.ShapeDtypeStruct((nseq, l, GROUP_WIDTH), F32)],
        compiler_params=_params("parallel", "arbitrary"),
        name=f"dilated_attn_{dil}",
    )(zg, zg, zg, zg, zg, q_g, k_g)


def _post_kernel(u_ref, o0_ref, o1_ref, o2_ref, l0_ref, l1_ref, l2_ref, ga_ref, gb_ref, x_ref,
                 mod_ref, n2_ref, wc_ref, bc_ref, wa_ref, wo_ref, x1_ref, h2_ref,
                 so1_ref, sl1_ref, so2_ref, sl2_ref):
    yc = jnp.dot(u_ref[0], wc_ref[...], preferred_element_type=F32) + bc_ref[...]

    def sequence_order(src_ref, dst_ref):
        dil, rows = src_ref.shape[1], src_ref.shape[2]
        nch = dst_ref.shape[0]
        for r in range(dil):
            src = src_ref[0, r].astype(F32)
            for c in range(nch):
                dst_ref[c, pl.ds(r, rows, stride=dil), :] = src[:, c * LANES:(c + 1) * LANES]
        return jnp.concatenate([dst_ref[c] for c in range(nch)], axis=1)

    o0, l0 = o0_ref[0].astype(F32), l0_ref[0]
    o1, l1 = sequence_order(o1_ref, so1_ref), sequence_order(l1_ref, sl1_ref)
    o2, l2 = sequence_order(o2_ref, so2_ref), sequence_order(l2_ref, sl2_ref)
    lm = jnp.maximum(jnp.maximum(l0, l1), l2)
    e0, e1, e2 = jnp.exp(l0 - lm), jnp.exp(l1 - lm), jnp.exp(l2 - lm)
    o = (e0 * o0 + e1 * o1 + e2 * o2) / (e0 + e1 + e2)
    ya = jnp.dot(o.astype(BF16), wa_ref[...], preferred_element_type=F32)
    merged = (jax.nn.sigmoid(ga_ref[0].astype(F32)) * yc + jax.nn.sigmoid(gb_ref[0].astype(F32)) * ya)
    out = jnp.dot(merged.astype(BF16), wo_ref[...], preferred_element_type=F32)
    x1 = x_ref[0] + mod_ref[0, 2:3, :] * out
    x1_ref[0] = x1
    h2 = _modulated_rmsnorm(x1, n2_ref[...], mod_ref[0, 3:4, :], mod_ref[0, 4:5, :])
    h2_ref[0] = h2.astype(h2_ref.dtype)


def _post(u2, os_, ls_, z3, off_g, x, mod3, norm2_g, wc, bc, wa, wo):
    b, s, d = x.shape
    ch = u2.shape[2]
    tm = 256
    gcol = off_g // d
    tok = lambda w: pl.BlockSpec((1, tm, w), lambda bi, i: (bi, i, 0))
    full = lambda a: pl.BlockSpec(a.shape, lambda bi, i: (0,) * a.ndim, pipeline_mode=pl.Buffered(1))

    def grp(a):
        if a.shape[1] == 1:
            return a.reshape(b, s, GROUP_WIDTH), tok(GROUP_WIDTH)
        dil = a.shape[1]
        return a, pl.BlockSpec((1, dil, tm // dil, GROUP_WIDTH), lambda bi, i: (bi, 0, i, 0))

    grp_args, grp_specs = zip(*[grp(a) for a in list(os_) + list(ls_)])
    n2 = norm2_g.reshape(1, d)
    bc2 = bc.reshape(1, d)
    return pl.pallas_call(
        _post_kernel,
        grid=(b, s // tm),
        in_specs=[tok(ch)] + list(grp_specs)
                 + [pl.BlockSpec((1, tm, d), lambda bi, i: (bi, i, gcol)),
                    pl.BlockSpec((1, tm, d), lambda bi, i: (bi, i, gcol + 1)),
                    tok(d),
                    pl.BlockSpec((1, 6, d), lambda bi, i: (bi, 0, 0)),
                    full(n2), full(wc), full(bc2), full(wa), full(wo)],
        out_specs=[tok(d), tok(d)],
        out_shape=[jax.ShapeDtypeStruct((b, s, d), F32), jax.ShapeDtypeStruct((b, s, d), BF16)],
        scratch_shapes=[pltpu.VMEM((GROUP_WIDTH // LANES, tm, LANES), F32)] * 4,
        compiler_params=_params("parallel", "parallel"),
        name="merge_out_proj",
    )(u2, *grp_args, z3, z3, x, mod3, n2, wc, bc2, wa, wo)


def _peer_scores_kernel(h_ref, wq_ref, keys_ref, s_ref):
    qp = jnp.dot(h_ref[...], wq_ref[...], preferred_element_type=F32).astype(BF16)
    nt = (((1,), (1,)), ((), ()))
    half = keys_ref.shape[2]
    for hp in range(keys_ref.shape[0]):
        s_ref[hp] = lax.dot_general(keys_ref[hp], qp[:, hp * half:(hp + 1) * half], nt,
                                    preferred_element_type=F32)


def _peer_scores(h2, wq, keys):
    t, d = h2.shape
    nhp, nk, half = keys.shape
    tm = 256
    return pl.pallas_call(
        _peer_scores_kernel,
        grid=(t // tm,),
        in_specs=[pl.BlockSpec((tm, d), lambda i: (i, 0)),
                  pl.BlockSpec(wq.shape, lambda i: (0, 0), pipeline_mode=pl.Buffered(1)),
                  pl.BlockSpec(keys.shape, lambda i: (0, 0, 0), pipeline_mode=pl.Buffered(1))],
        out_specs=pl.BlockSpec((nhp, nk, tm), lambda i: (0, 0, i)),
        out_shape=jax.ShapeDtypeStruct((nhp, nk, t), F32),
        compiler_params=_params("parallel"),
        name="peer_scores",
    )(h2, wq, keys)


def _top16(s, tie_safe):
    key = lax.broadcasted_iota(jnp.int32, s.shape, 0).astype(F32)
    slot = lax.broadcasted_iota(jnp.int32, (PEER_TOPK, s.shape[1]), 0)
    rank = jnp.full(s.shape, float(PEER_TOPK), F32)
    vals = jnp.zeros((PEER_TOPK, s.shape[1]), F32)
    for k in range(PEER_TOPK):
        m = jnp.max(s, axis=0, keepdims=True)
        sel = s == m
        if tie_safe:
            sel = key == jnp.min(jnp.where(sel, key, float(PEER_NKEYS)), axis=0, keepdims=True)
        rank = jnp.where(sel, float(k), rank)
        s = jnp.where(sel, -jnp.inf, s)
        vals = jnp.where(slot == k, m, vals)
    return vals, rank


def _ranks_are_a_top16(rank):
    expect = float(PEER_NKEYS * PEER_TOPK - PEER_TOPK * (PEER_TOPK + 1) // 2)
    return jnp.sum(rank, axis=0, keepdims=True) == expect


def _staircase_lengths(v1, v2, tie_safe):
    tl = v1.shape[1]
    j8 = lax.broadcasted_iota(jnp.int32, (8, tl), 0)
    j16 = lax.broadcasted_iota(jnp.int32, (PEER_TOPK, tl), 0)
    cands = [v1[0:1, :] + v2]
    flats = [j16]
    for i in range(1, 8):
        cands.append(jnp.where(j8 < PEER_TOPK // (i + 1), v1[i:i + 1, :] + v2[0:8, :], -jnp.inf))
        flats.append(j8 + i * PEER_TOPK)
    cands.append(v1[8:16, :] + v2[0:1, :])
    flats.append((j8 + 8) * PEER_TOPK)
    cand = jnp.concatenate(cands, axis=0)
    flat = jnp.concatenate(flats, axis=0).astype(F32)
    big = float(PEER_TOPK * PEER_TOPK)
    in_staircase = cand > -jnp.inf
    z = jnp.zeros((1, tl), F32)
    top0 = None
    for k in range(PEER_TOPK):
        m = jnp.max(cand, axis=0, keepdims=True)
        sel = cand == m
        if tie_safe:
            sel = flat == jnp.min(jnp.where(sel, flat, big), axis=0, keepdims=True)
        cand = jnp.where(sel, -jnp.inf, cand)
        if k == 0:
            top0 = m
        z = z + jnp.exp(m - top0)
    picked = jnp.where(jnp.logical_and(in_staircase, cand == -jnp.inf), 1.0, 0.0)
    ok = jnp.sum(picked, axis=0, keepdims=True) == float(PEER_TOPK)
    lens_lo = jnp.zeros((8, tl), F32)
    lens_lo = jnp.where(j8 == 0, jnp.sum(picked[0:16], axis=0, keepdims=True), lens_lo)
    for i in range(1, 8):
        lens_lo = jnp.where(j8 == i, jnp.sum(picked[8 + 8 * i:16 + 8 * i], axis=0, keepdims=True), lens_lo)
    return jnp.concatenate([lens_lo, picked[72:80]], axis=0), z, ok


SELECT_HEADS_PER_BLOCK = 4


def _select_heads(s_ref, e1_ref, len_ref, r2_ref, e2_ref, h0, tie_safe):
    ok = None
    for dh in range(SELECT_HEADS_PER_BLOCK):
        h = h0 + dh
        s1 = s_ref[2 * h]
        s2 = s_ref[2 * h + 1]
        v1, rank1 = _top16(s1, tie_safe)
        v2, rank2 = _top16(s2, tie_safe)
        lens, z, good = _staircase_lengths(v1, v2, tie_safe)
        for rank in (rank1, rank2):
            good = jnp.logical_and(good, _ranks_are_a_top16(rank))
        ok = good if ok is None else jnp.logical_and(ok, good)
        sel_len = jnp.zeros(rank1.shape, F32)
        for i in range(PEER_TOPK):
            sel_len = jnp.where(rank1 == float(i), lens[i:i + 1, :], sel_len)
        e1_ref[h] = jnp.exp(s1 - v1[0:1, :])
        len_ref[h] = sel_len
        r2_ref[h] = rank2.astype(r2_ref.dtype)
        e2_ref[h] = (jnp.exp(s2 - v2[0:1, :]) / z).astype(e2_ref.dtype)
    return ok


def _peer_select_kernel(s_ref, e1_ref, len_ref, r2_ref, e2_ref):
    def block(i, carry):
        h0 = i * SELECT_HEADS_PER_BLOCK
        ok = _select_heads(s_ref, e1_ref, len_ref, r2_ref, e2_ref, h0, tie_safe=False)

        @pl.when(jnp.sum(jnp.where(ok, 0.0, 1.0)) > 0.0)
        def _():
            _select_heads(s_ref, e1_ref, len_ref, r2_ref, e2_ref, h0, tie_safe=True)

        return carry

    lax.fori_loop(0, PEER_HEADS // SELECT_HEADS_PER_BLOCK, block, 0)


def _peer_select(scores_t):
    nhp, nk, t = scores_t.shape
    tl = 128
    out_spec = pl.BlockSpec((PEER_HEADS, nk, tl), lambda i: (0, 0, i))
    shp = lambda dt: jax.ShapeDtypeStruct((PEER_HEADS, nk, t), dt)
    return pl.pallas_call(
        _peer_select_kernel,
        grid=(t // tl,),
        in_specs=[pl.BlockSpec((nhp, nk, tl), lambda i: (0, 0, i))],
        out_specs=[out_spec] * 4,
        out_shape=[shp(F32), shp(F32), shp(BF16), shp(BF16)],
        compiler_params=_params("parallel"),
        name="peer_select",
    )(scores_t)


def _peer_dense_kernel(h_ref, wu_ref, wdt_ref, e1_ref, len_ref, r2_ref, e2_ref, x1_ref, mod_ref,
                       o_ref, acc_ref):
    j = pl.program_id(1)

    @pl.when(j == 0)
    def _():
        acc_ref[...] = jnp.zeros_like(acc_ref)

    nt = (((1,), (1,)), ((), ()))
    pre = lax.dot_general(wu_ref[...], h_ref[...], nt, preferred_element_type=F32)
    act = (0.5 * pre * (1.0 + lax.erf(pre * (0.5 ** 0.5)))).astype(BF16)
    nk = r2_ref.shape[1]
    parts = []
    for c in range(wu_ref.shape[0] // nk):
        gate = jnp.zeros((nk, h_ref.shape[0]), BF16)
        for h in range(PEER_HEADS):
            sel_len = len_ref[h, c:c + 1, :].astype(BF16)
            e1 = e1_ref[h, c:c + 1, :].astype(BF16)
            gate = gate + jnp.where(r2_ref[h] < sel_len, e2_ref[h], jnp.zeros((), BF16)) * e1
        parts.append(act[c * nk:(c + 1) * nk, :] * gate)
    p = jnp.concatenate(parts, axis=0)
    acc_ref[...] += jnp.dot(wdt_ref[...], p, preferred_element_type=F32)

    @pl.when(j == pl.num_programs(1) - 1)
    def _():
        o_ref[...] = x1_ref[...] + mod_ref[0, 5:6, :] * acc_ref[...].T


def _peer_dense(h2, wu, wdt, e1t, lent, r2t, e2t, x1, mod3):
    t, d = h2.shape
    ne = wu.shape[0]
    nk = r2t.shape[1]
    tl = 512
    te = 8 * nk
    small = pl.BlockSpec((PEER_HEADS, 8, tl), lambda i, j: (0, j, i))
    table = pl.BlockSpec((PEER_HEADS, nk, tl), lambda i, j: (0, 0, i))
    per_b = t // mod3.shape[0] // tl
    once = pl.Buffered(1)
    return pl.pallas_call(
        _peer_dense_kernel,
        grid=(t // tl, ne // te),
        in_specs=[pl.BlockSpec((tl, d), lambda i, j: (i, 0), pipeline_mode=once),
                  pl.BlockSpec((te, d), lambda i, j: (j, 0)),
                  pl.BlockSpec((d, te), lambda i, j: (0, j)),
                  small, small, table, table,
                  pl.BlockSpec((tl, d), lambda i, j: (i, 0), pipeline_mode=once),
                  pl.BlockSpec((1, 6, d), lambda i, j: (i // per_b, 0, 0))],
        out_specs=pl.BlockSpec((tl, d), lambda i, j: (i, 0)),
        out_shape=jax.ShapeDtypeStruct((t, d), F32),
        scratch_shapes=[pltpu.VMEM((d, tl), F32)],
        compiler_params=_params("parallel", "arbitrary"),
        name="peer_dense",
    )(h2, wu, wdt, e1t, lent, r2t, e2t, x1, mod3)


def kernel(x, c, norm1_g, norm2_g, w_ada, b_ada, w_in, conv_dw, conv_db, conv_ln_g, conv_ln_b,
           w_conv_out, b_conv_out, q_norm_g, k_norm_g, w_attn_o, w_out,
           peer_w_q, peer_sub_keys, peer_w_up, peer_w_down):
    b, s, d = x.shape
    t = b * s
    depth = w_ada.shape[0]
    conv_ch = conv_dw.shape[2]
    in_cols = w_in.shape[2]
    tn = GROUP_WIDTH
    ngrp = len(DILATIONS)
    glu_t = 2 * conv_ch // tn
    qkv_t = lambda gi: [glu_t + gi, glu_t + ngrp + gi, glu_t + 2 * ngrp + gi]
    gate_t = list(range(glu_t + 3 * ngrp, in_cols // tn))
    main_t = list(range(glu_t)) + gate_t + qkv_t(0)
    off_g = glu_t * tn
    for l in range(depth):
        mod3 = _ada(c, w_ada[l], b_ada[l]).reshape(b, 6, d)
        hs = _prenorm(x, norm1_g[l], mod3)
        z = _in_proj(hs[0].reshape(t, d), w_in[l], IN_PROJ_ROWS, tn, main_t, "in_proj")
        z3 = z.reshape(b, s, len(main_t) * tn)
        u2 = _conv_branch(z3, conv_dw[l], conv_db[l], conv_ln_g[l], conv_ln_b[l])
        q_g = q_norm_g[l].reshape(ngrp, 1, GROUP_WIDTH)
        k_g = k_norm_g[l].reshape(ngrp, 1, GROUP_WIDTH)
        outs, lses = [], []
        for gi, dil in enumerate(DILATIONS):
            if dil == 1:
                zg, cq = z3, len(main_t) - 3
            else:
                zg = _in_proj(hs[gi].reshape(t, d), w_in[l], IN_PROJ_ROWS, tn, qkv_t(gi), f"in_proj_dil{dil}")
                zg, cq = zg.reshape(b * dil, s // dil, 3 * tn), 0
            o, lse = _attn_group(zg, q_g[gi], k_g[gi], dil, cq, cq + 1, cq + 2)
            outs.append(o.reshape(b, dil, s // dil, GROUP_WIDTH))
            lses.append(lse.reshape(b, dil, s // dil, GROUP_WIDTH))
        x1, h2 = _post(u2, outs, lses, z3, off_g, x, mod3, norm2_g[l],
                       w_conv_out[l].astype(BF16), b_conv_out[l], w_attn_o[l].astype(BF16),
                       w_out[l].astype(BF16))
        h2 = h2.reshape(t, d)
        keys = peer_sub_keys[l].reshape(2 * PEER_HEADS, PEER_NKEYS, -1).astype(BF16)
        scores_t = _peer_scores(h2, peer_w_q[l].astype(BF16), keys)
        e1t, lent, r2t, e2t = _peer_select(scores_t)
        x = _peer_dense(h2, peer_w_up[l].astype(BF16), peer_w_down[l].T.astype(BF16),
                        e1t, lent, r2t, e2t, x1.reshape(t, d), mod3).reshape(b, s, d)
    return x
```
